```python
import jax, jax.numpy as jnp
from jax import lax
import numpy as np

D_MODEL = 2048
BATCH = 4
SEQ = 8192
DEPTH = 1

CHUNK = 64
LEFT_CHUNKS = 8
ATT_HEADS = 16
ATT_HEAD_DIM = 64
ATT_WIDTH = ATT_HEADS * ATT_HEAD_DIM
MAX_PAST_DIST = 256
REL_TABLE = MAX_PAST_DIST + CHUNK
RWKV_HEADS = 16
RWKV_HEAD_DIM = 64
RWKV_WIDTH = RWKV_HEADS * RWKV_HEAD_DIM
DECAY_LORA = 96
ICLR_LORA = 96
GATE_LORA = 256
SHIFT_WIDTH = 3 * RWKV_WIDTH + DECAY_LORA + ICLR_LORA + GATE_LORA
D_IN = 3 * ATT_WIDTH + SHIFT_WIDTH + 2 * D_MODEL
N_EXPERTS = 32
TOP_K = 4
D_EXPERT = D_MODEL
SWIGLU_LIMIT = 7.0
SWIGLU_ALPHA = 1.702
MOE_BLOCK = 256
LN_EPS = 1e-5
GN_EPS = 64e-5
DEEPNORM_ALPHA = (2 * DEPTH) ** 0.25
DEEPNORM_BETA = (8 * DEPTH) ** -0.25

kernel_name = 'hybrid_chunkattn_rwkv7_moe_deepnorm'


def _layer_norm(x, w, b):
    xf = x.astype(jnp.float32)
    mu = jnp.mean(xf, axis=-1, keepdims=True)
    var = jnp.mean(jnp.square(xf - mu), axis=-1, keepdims=True)
    return ((xf - mu) * lax.rsqrt(var + LN_EPS) * w + b).astype(x.dtype)


def _chunk_attention(q, k, v, rel_bias):
    B, S, H, Dh = q.shape
    n_chunks = S // CHUNK
    pad = LEFT_CHUNKS * CHUNK
    band = pad + CHUNK
    kp = jnp.pad(k, ((0, 0), (pad, 0), (0, 0), (0, 0)))
    vp = jnp.pad(v, ((0, 0), (pad, 0), (0, 0), (0, 0)))
    qi = jnp.arange(CHUNK)[:, None]
    kj = jnp.arange(band)[None, :]
    dist = qi + pad - kj
    idx = jnp.minimum(dist, MAX_PAST_DIST) + (CHUNK - 1)
    bias = rel_bias.astype(jnp.float32)[:, idx]
    scale = Dh ** -0.5
    key_slot = jnp.arange(band)

    def one_chunk(c):
        start = c * CHUNK
        qc = lax.dynamic_slice_in_dim(q, start, CHUNK, axis=1)
        kc = lax.dynamic_slice_in_dim(kp, start, band, axis=1)
        vc = lax.dynamic_slice_in_dim(vp, start, band, axis=1)
        s = jnp.einsum('bqhd,bkhd->bhqk', qc, kc).astype(jnp.float32) * scale + bias
        valid = key_slot + start >= pad
        s = jnp.where(valid, s, jnp.finfo(jnp.float32).min)
        p = jax.nn.softmax(s, axis=-1)
        return jnp.einsum('bhqk,bkhd->bqhd', p.astype(vc.dtype), vc)

    out = lax.map(one_chunk, jnp.arange(n_chunks))
    return jnp.moveaxis(out, 0, 1).reshape(B, S, H * Dh)


def _rwkv7_scan(r, decay, k, v, a, b):
    B, S, H, N = r.shape
    xs = tuple(jnp.moveaxis(t, 1, 0) for t in (r, decay, k, v, a, b))

    def step(state, inp):
        r_t, w_t, k_t, v_t, a_t, b_t = inp
        sa = jnp.einsum('bhvk,bhk->bhv', state, a_t)
        state = (state * w_t[:, :, None, :] + sa[..., None] * b_t[:, :, None, :]
                 + v_t[..., None] * k_t[:, :, None, :])
        y = jnp.einsum('bhvk,bhk->bhv', state, r_t)
        return state, y

    state0 = jnp.zeros((B, H, N, N), jnp.float32)
    _, ys = lax.scan(step, state0, xs)
    return jnp.moveaxis(ys, 0, 1)


def _rwkv7_time_mix(r, k, v, wd, ad, gd, w0, w_up, a0, a_up, g_up, k_k, k_a, r_k, lnx_w, lnx_b):
    B, S, _ = r.shape
    H, N = RWKV_HEADS, RWKV_HEAD_DIM
    out_dtype = r.dtype
    f32 = jnp.float32
    r, k, v = r.astype(f32), k.astype(f32), v.astype(f32)
    w = -jax.nn.softplus(-(w0 + jnp.tanh(wd) @ w_up).astype(f32)) - 0.5
    decay = jnp.exp(-jnp.exp(w))
    a = jax.nn.sigmoid((a0 + ad @ a_up).astype(f32))
    g = (jax.nn.sigmoid(gd) @ g_up).astype(f32)
    heads = lambda t: t.reshape(B, S, H, N)
    kk = heads(k * k_k)
    kk = kk / jnp.maximum(jnp.sqrt(jnp.sum(jnp.square(kk), axis=-1, keepdims=True)), 1e-12)
    k = k * (1.0 + (a - 1.0) * k_a)
    rh, kh, vh, ah = heads(r), heads(k), heads(v), heads(a)
    y = _rwkv7_scan(rh, heads(decay), kh, vh, -kk, kk * ah)
    mu = jnp.mean(y, axis=-1, keepdims=True)
    var = jnp.mean(jnp.square(y - mu), axis=-1, keepdims=True)
    y = ((y - mu) * lax.rsqrt(var + GN_EPS)).reshape(B, S, H * N) * lnx_w + lnx_b
    bonus = (jnp.sum(rh * kh * r_k, axis=-1, keepdims=True) * vh).reshape(B, S, H * N)
    return ((y + bonus) * g).astype(out_dtype)


def _token_mixing(x, w_in, rel_bias, shift_mu, w0, w_up, a0, a_up, g_up, k_k, k_a, r_k,
                  lnx_w, lnx_b, proj_a, proj_b, w_out):
    B, S, D = x.shape
    proj = jnp.einsum('bsd,de->bse', x, w_in)
    o1 = 3 * ATT_WIDTH
    o2 = o1 + SHIFT_WIDTH
    q, k, v = jnp.split(proj[..., :o1], 3, axis=-1)
    seg = proj[..., o1:o2]
    prev = jnp.pad(seg, ((0, 0), (1, 0), (0, 0)))[:, :-1]
    seg = seg + (prev - seg) * shift_mu
    W = RWKV_WIDTH
    r_b, k_b, v_b, wd, ad, gd = jnp.split(
        seg, [W, 2 * W, 3 * W, 3 * W + DECAY_LORA, 3 * W + DECAY_LORA + ICLR_LORA], axis=-1)
    gate_a, gate_b = jnp.split(jax.nn.sigmoid(proj[..., o2:]), 2, axis=-1)
    heads = lambda t: t.reshape(B, S, ATT_HEADS, ATT_HEAD_DIM)
    y_a = _chunk_attention(heads(q), heads(k), heads(v), rel_bias)
    y_b = _rwkv7_time_mix(r_b, k_b, v_b, wd, ad, gd, w0, w_up, a0, a_up, g_up,
                          k_k, k_a, r_k, lnx_w, lnx_b)
    merged = (gate_a * jnp.einsum('bsc,cd->bsd', y_a, proj_a)
              + gate_b * jnp.einsum('bsc,cd->bsd', y_b, proj_b))
    return jnp.einsum('bsd,de->bse', merged, w_out)


def _moe(h, w_router, b_router, w_gu, b_gu, w_down, b_down):
    T, D = h.shape
    logits = jnp.einsum('td,de->te', h, w_router).astype(jnp.float32) + b_router.astype(jnp.float32)
    top_v, top_i = lax.top_k(logits, TOP_K)
    gate = jax.nn.softmax(top_v, axis=-1)
    M = T * TOP_K
    slot_e = top_i.reshape(M).astype(jnp.int32)
    slot_t = jnp.arange(M, dtype=jnp.int32) // TOP_K
    slot_g = gate.reshape(M)
    order = jnp.argsort(slot_e)
    e_sorted = slot_e[order]
    counts = jnp.zeros((N_EXPERTS,), jnp.int32).at[slot_e].add(1)
    starts = jnp.cumsum(counts) - counts
    padded = (counts + MOE_BLOCK - 1) // MOE_BLOCK * MOE_BLOCK
    pad_ends = jnp.cumsum(padded)
    pad_starts = pad_ends - padded
    dest = pad_starts[e_sorted] + (jnp.arange(M, dtype=jnp.int32) - starts[e_sorted])
    n_blocks = (M + N_EXPERTS * (MOE_BLOCK - 1) + MOE_BLOCK - 1) // MOE_BLOCK
    R = n_blocks * MOE_BLOCK
    row_tok = jnp.zeros((R,), jnp.int32).at[dest].set(slot_t[order])
    row_w = jnp.zeros((R,), jnp.float32).at[dest].set(slot_g[order])
    blk_e = jnp.minimum(jnp.searchsorted(pad_ends, jnp.arange(n_blocks) * MOE_BLOCK, side='right'),
                        N_EXPERTS - 1)

    def body(acc, inp):
        e, toks, wts = inp
        xb = h[toks]
        gu = xb @ w_gu[e] + b_gu[e]
        g, u = jnp.split(gu, 2, axis=-1)
        g = jnp.minimum(g, SWIGLU_LIMIT)
        u = jnp.clip(u, -SWIGLU_LIMIT, SWIGLU_LIMIT)
        act = (u + 1.0) * (g * jax.nn.sigmoid(SWIGLU_ALPHA * g))
        y = act @ w_down[e] + b_down[e]
        acc = acc.at[toks].add((y * wts[:, None]).astype(acc.dtype))
        return acc, None

    out, _ = lax.scan(body, jnp.zeros_like(h),
                      (blk_e, row_tok.reshape(n_blocks, MOE_BLOCK), row_w.reshape(n_blocks, MOE_BLOCK)))
    return out


def setup_inputs(seed: int = 0) -> dict:
    key = jax.random.key(seed)
    ks = jax.random.split(key, 27)
    f32 = jnp.float32
    L = DEPTH
    nrm = lambda k, shape, scale: jax.random.normal(k, shape, f32) * scale
    F = D_EXPERT
    return {
        'x': nrm(ks[0], (BATCH, SEQ, D_MODEL), 1.0),
        'w_in': nrm(ks[1], (L, D_MODEL, D_IN), D_MODEL ** -0.5),
        'rel_bias': nrm(ks[2], (L, ATT_HEADS, REL_TABLE), 0.2),
        'shift_mu': jax.random.uniform(ks[3], (L, SHIFT_WIDTH), f32, 0.1, 0.9),
        'w0': jax.random.uniform(ks[4], (L, RWKV_WIDTH), f32, -4.0, 0.0),
        'w_up': nrm(ks[5], (L, DECAY_LORA, RWKV_WIDTH), 0.5 * DECAY_LORA ** -0.5),
        'a0': nrm(ks[6], (L, RWKV_WIDTH), 0.3),
        'a_up': nrm(ks[7], (L, ICLR_LORA, RWKV_WIDTH), 0.5 * ICLR_LORA ** -0.5),
        'g_up': nrm(ks[8], (L, GATE_LORA, RWKV_WIDTH), GATE_LORA ** -0.5),
        'k_k': 0.85 + nrm(ks[9], (L, RWKV_WIDTH), 0.05),
        'k_a': 1.0 + nrm(ks[10], (L, RWKV_WIDTH), 0.05),
        'r_k': nrm(ks[11], (L, RWKV_HEADS, RWKV_HEAD_DIM), 0.1),
        'lnx_w': 1.0 + nrm(ks[12], (L, RWKV_WIDTH), 0.05),
        'lnx_b': nrm(ks[13], (L, RWKV_WIDTH), 0.01),
        'proj_a': nrm(ks[14], (L, ATT_WIDTH, D_MODEL), DEEPNORM_BETA * ATT_WIDTH ** -0.5),
        'proj_b': nrm(ks[15], (L, RWKV_WIDTH, D_MODEL), DEEPNORM_BETA * RWKV_WIDTH ** -0.5),
        'w_out': nrm(ks[16], (L, D_MODEL, D_MODEL), DEEPNORM_BETA * D_MODEL ** -0.5),
        'ln1_w': 1.0 + nrm(ks[17], (L, D_MODEL), 0.05),
        'ln1_b': nrm(ks[18], (L, D_MODEL), 0.01),
        'w_router': nrm(ks[19], (L, D_MODEL, N_EXPERTS), D_MODEL ** -0.5),
        'b_router': nrm(ks[20], (L, N_EXPERTS), 0.01),
        'w_gu': nrm(ks[21], (L, N_EXPERTS, D_MODEL, 2 * F), D_MODEL ** -0.5),
        'b_gu': nrm(ks[22], (L, N_EXPERTS, 2 * F), 0.01),
        'w_down': nrm(ks[23], (L, N_EXPERTS, F, D_MODEL), DEEPNORM_BETA * F ** -0.5),
        'b_down': nrm(ks[24], (L, N_EXPERTS, D_MODEL), 0.01),
        'ln2_w': 1.0 + nrm(ks[25], (L, D_MODEL), 0.05),
        'ln2_b': nrm(ks[26], (L, D_MODEL), 0.01),
    }


def reference(x, w_in, rel_bias, shift_mu, w0, w_up, a0, a_up, g_up, k_k, k_a, r_k, lnx_w, lnx_b,
              proj_a, proj_b, w_out, ln1_w, ln1_b, w_router, b_router, w_gu, b_gu, w_down, b_down,
              ln2_w, ln2_b):
    h = x
    B, S, D = x.shape
    for l in range(DEPTH):
        mix = _token_mixing(h, w_in[l], rel_bias[l], shift_mu[l], w0[l], w_up[l], a0[l], a_up[l],
                            g_up[l], k_k[l], k_a[l], r_k[l], lnx_w[l], lnx_b[l],
                            proj_a[l], proj_b[l], w_out[l])
        h = _layer_norm(DEEPNORM_ALPHA * h + mix, ln1_w[l], ln1_b[l])
        ffn = _moe(h.reshape(B * S, D), w_router[l], b_router[l], w_gu[l], b_gu[l],
                   w_down[l], b_down[l]).reshape(B, S, D)
        h = _layer_norm(DEEPNORM_ALPHA * h + ffn, ln2_w[l], ln2_b[l])
    return h
```

```python
import functools

import jax
import jax.numpy as jnp
from jax import lax
from jax.experimental import pallas as pl
from jax.experimental.pallas import tpu as pltpu

f32 = jnp.float32
bf16 = jnp.bfloat16
i32 = jnp.int32

CHUNK = 64
LEFT_CHUNKS = 8
HEAD_DIM = 64
ATT_WIDTH = 1024
MAX_PAST_DIST = 256
REL_TABLE = MAX_PAST_DIST + CHUNK
RWKV_WIDTH = 1024
DECAY_LORA = 96
ICLR_LORA = 96
GATE_LORA = 256
TOP_K = 4
SWIGLU_LIMIT = 7.0
SWIGLU_ALPHA = 1.702
LN_EPS = 1e-5
GN_EPS = 64e-5
DEPTH = 1
DEEPNORM_ALPHA = (2 * DEPTH) ** 0.25

LANES = 128
PAIR = 2 * HEAD_DIM
BF16_SUBLANES = 16
VMEM_LIMIT = 56 * 1024 * 1024

NEG = -1e30
HI = lax.Precision.HIGHEST

LORA_PAD = 512
COL_GATE = 0
COL_Q = 4096
COL_K = COL_Q + ATT_WIDTH
COL_V = COL_K + ATT_WIDTH
COL_R = COL_V + ATT_WIDTH
COL_KR = COL_R + RWKV_WIDTH
COL_VR = COL_KR + RWKV_WIDTH
COL_LORA = COL_VR + RWKV_WIDTH
PROJ_COLS = COL_LORA + LORA_PAD

ATT_TQ = 256
RWKV_TR = 256
MERGE_TM = 256
MOE_RB = 512
MOE_TF = 512
COMBINE_TM = 256


def _dot(a, b, precision=None):
    return jnp.dot(a, b, preferred_element_type=f32, precision=precision)


def _dot_nt(a, b):
    return lax.dot_general(a, b, (((1,), (1,)), ((), ())), preferred_element_type=f32)


def _dot_tn(a, b):
    return lax.dot_general(a, b, (((0,), (0,)), ((), ())), preferred_element_type=f32)


def _inproj_kernel(x_ref, w_ref, o_ref, xb_ref):
    @pl.when(pl.program_id(1) == 0)
    def _():
        xb_ref[...] = x_ref[...].astype(bf16)

    o_ref[...] = _dot(xb_ref[...], w_ref[...]).astype(o_ref.dtype)


def _in_proj(x2, w, tm, tn):
    M, K = x2.shape
    N = w.shape[1]
    return pl.pallas_call(
        _inproj_kernel,
        out_shape=jax.ShapeDtypeStruct((M, N), bf16),
        grid=(M // tm, N // tn),
        in_specs=[pl.BlockSpec((tm, K), lambda i, j: (i, 0)),
                  pl.BlockSpec((K, tn), lambda i, j: (0, j))],
        out_specs=pl.BlockSpec((tm, tn), lambda i, j: (i, j)),
        scratch_shapes=[pltpu.VMEM((tm, K), bf16)],
        compiler_params=pltpu.CompilerParams(
            dimension_semantics=("parallel", "arbitrary"), vmem_limit_bytes=VMEM_LIMIT),
        name="in_proj",
    )(x2, w)


def _attn_bias_table(rel_bias):
    tq = ATT_TQ
    qi = jnp.arange(tq)[:, None]
    kj = jnp.arange(3 * tq)[None, :]
    dist = qi + 2 * tq - kj
    dchunk = (qi + 2 * tq) // CHUNK - kj // CHUNK
    ok = (dchunk >= 0) & (dchunk <= LEFT_CHUNKS)
    idx = jnp.clip(jnp.minimum(dist, MAX_PAST_DIST) + (CHUNK - 1), 0, REL_TABLE - 1)
    b = rel_bias.astype(f32)[:, idx]
    return jnp.where(ok[None], b, NEG)


def _attn_kernel(q_ref, k0_ref, k1_ref, k2_ref, v0_ref, v1_ref, v2_ref, bias_ref, o_ref):
    qt = pl.program_id(1)
    tq = q_ref.shape[0]
    k_refs = (k0_ref, k1_ref, k2_ref)
    v_refs = (v0_ref, v1_ref, v2_ref)
    outs = []
    for h in range(2):
        sl = slice(h * HEAD_DIM, (h + 1) * HEAD_DIM)
        q = q_ref[:, sl] * (HEAD_DIM ** -0.5)
        parts = []
        for j in range(3):
            s = _dot_nt(q, k_refs[j][:, sl]) + bias_ref[h, :, j * tq:(j + 1) * tq]
            if j < 2:
                s = jnp.where(qt - 2 + j >= 0, s, NEG)
            parts.append(s)
        m = jnp.maximum(jnp.maximum(jnp.max(parts[0], axis=-1, keepdims=True),
                                    jnp.max(parts[1], axis=-1, keepdims=True)),
                        jnp.max(parts[2], axis=-1, keepdims=True))
        acc = jnp.zeros((tq, HEAD_DIM), f32)
        l = jnp.zeros((tq, 1), f32)
        for j in range(3):
            p = jnp.exp(parts[j] - m)
            l = l + jnp.sum(p, axis=-1, keepdims=True)
            acc = acc + _dot(p.astype(bf16), v_refs[j][:, sl])
        outs.append(acc / l)
    o_ref[...] = jnp.concatenate(outs, axis=1).astype(o_ref.dtype)


def _attention(proj, bias, B, S):
    T = B * S
    tq = ATT_TQ
    nt = S // tq
    qcol, kcol, vcol = COL_Q // PAIR, COL_K // PAIR, COL_V // PAIR

    def kv_spec(col, back):
        return pl.BlockSpec((tq, PAIR),
                            lambda b, t, h: (b * nt + jnp.maximum(t - back, 0), col + h))

    return pl.pallas_call(
        _attn_kernel,
        out_shape=jax.ShapeDtypeStruct((T, ATT_WIDTH), bf16),
        grid=(B, nt, ATT_WIDTH // PAIR),
        in_specs=[pl.BlockSpec((tq, PAIR), lambda b, t, h: (b * nt + t, qcol + h)),
                  kv_spec(kcol, 2), kv_spec(kcol, 1), kv_spec(kcol, 0),
                  kv_spec(vcol, 2), kv_spec(vcol, 1), kv_spec(vcol, 0),
                  pl.BlockSpec((2, tq, 3 * tq), lambda b, t, h: (h, 0, 0))],
        out_specs=pl.BlockSpec((tq, PAIR), lambda b, t, h: (b * nt + t, h)),
        compiler_params=pltpu.CompilerParams(
            dimension_semantics=("parallel", "parallel", "parallel"), vmem_limit_bytes=VMEM_LIMIT),
        name="chunk_attn",
    )(proj, proj, proj, proj, proj, proj, proj, bias)


def _rwkv_kernel(r_ref, k_ref, v_ref, lo_ref, rp_ref, kp_ref, vp_ref, lop_ref,
                 mur_ref, muk_ref, muv_ref, mulo_ref, w0_ref, a0_ref, kkp_ref, kap_ref, rkp_ref,
                 lnw_ref, lnb_ref, wup_ref, aup_ref, gup_ref,
                 o_ref,
                 st_ref, r_s, lw_s, k_s, v_s, a_s, b_s, g_s, bon_s):
    t = pl.program_id(2)
    tr = r_ref.shape[0]
    C = CHUNK

    @pl.when(t == 0)
    def _():
        st_ref[...] = jnp.zeros_like(st_ref)

    not_first = jnp.where(t == 0, 0.0, 1.0).astype(f32)
    row = lax.broadcasted_iota(i32, (tr, 1), 0)

    def shift(p_ref, prev_ref, mu_ref):
        p = p_ref[...].astype(f32)
        prev_row = prev_ref[BF16_SUBLANES - 1:BF16_SUBLANES, :].astype(f32) * not_first
        prev = jnp.where(row == 0, prev_row, pltpu.roll(p, 1, 0))
        return p + (prev - p) * mu_ref[...]

    r = shift(r_ref, rp_ref, mur_ref)
    k = shift(k_ref, kp_ref, muk_ref)
    v = shift(v_ref, vp_ref, muv_ref)
    lo = shift(lo_ref, lop_ref, mulo_ref)
    wd = lo[:, 0:LANES]
    ad = lo[:, LANES:2 * LANES]
    gd = lo[:, 2 * LANES:]

    ri = lax.broadcasted_iota(i32, (PAIR, PAIR), 0)
    ci = lax.broadcasted_iota(i32, (PAIR, PAIR), 1)
    same_head = (ri // HEAD_DIM) == (ci // HEAD_DIM)
    head_ones = jnp.where(same_head, 1.0, 0.0).astype(f32)
    head_avg = head_ones * (1.0 / HEAD_DIM)

    z = w0_ref[...] + _dot(jnp.tanh(wd).astype(bf16), wup_ref[...])
    softplus_negz = jnp.maximum(-z, 0.0) + jnp.log(1.0 + jnp.exp(-jnp.abs(z)))
    lw = -jnp.exp(-softplus_negz - 0.5)
    alr = jax.nn.sigmoid(a0_ref[...] + _dot(ad.astype(bf16), aup_ref[...]))
    g = _dot(jax.nn.sigmoid(gd).astype(bf16), gup_ref[...])
    kk = k * kkp_ref[...]
    kk = kk / jnp.maximum(jnp.sqrt(_dot(kk * kk, head_ones, HI)), 1e-12)
    k2 = k * (1.0 + (alr - 1.0) * kap_ref[...])
    bonus = _dot(r * k2 * rkp_ref[...], head_ones, HI) * v

    r_s[...] = r
    lw_s[...] = lw
    k_s[...] = k2
    v_s[...] = v
    a_s[...] = -kk
    b_s[...] = kk * alr
    g_s[...] = g
    bon_s[...] = bonus

    lane = lax.broadcasted_iota(i32, (1, PAIR), 1)
    head0 = lane < HEAD_DIM
    tri_incl = jnp.where(lax.broadcasted_iota(i32, (C, C), 1) <= lax.broadcasted_iota(i32, (C, C), 0),
                         1.0, 0.0).astype(f32)
    strict = ci < ri
    incl = ci <= ri
    eye = ci == ri
    lnw = lnw_ref[...]
    lnb = lnb_ref[...]

    def stack(x):
        return jnp.concatenate([jnp.where(head0, x, 0.0), jnp.where(head0, 0.0, x)], axis=0)

    def chunk(c, carry):
        rows = pl.ds(pl.multiple_of(c * C, C), C)
        lwc = lw_s[rows, :]
        rc, kc, vc, ac, bc = r_s[rows, :], k_s[rows, :], v_s[rows, :], a_s[rows, :], b_s[rows, :]
        L = _dot(tri_incl, lwc, HI)
        LC = L[C - 1:C, :]
        e_pos = jnp.exp(L)
        e_neg = jnp.exp(-L)
        e_prev = jnp.exp(L - lwc)
        e_end = jnp.exp(LC - L)
        a_t = stack(ac * e_prev)
        r_t = stack(rc * e_pos)
        b_t = stack(bc * e_neg)
        k_t = stack(kc * e_neg)
        b_h = stack(bc * e_end)
        k_h = stack(kc * e_end)
        v_st = stack(vc)

        A = _dot_nt(jnp.concatenate([a_t, r_t], axis=0).astype(bf16),
                    jnp.concatenate([b_t, k_t], axis=0).astype(bf16))
        a_ab = jnp.where(strict, A[0:PAIR, 0:PAIR], 0.0)
        a_ak = jnp.where(strict, A[0:PAIR, PAIR:], 0.0)
        a_rb = jnp.where(incl, A[PAIR:, 0:PAIR], 0.0)
        a_rk = jnp.where(incl, A[PAIR:, PAIR:], 0.0)

        p = a_ab
        tinv = jnp.where(eye, 1.0, 0.0).astype(f32) + a_ab
        for _ in range(5):
            pb = p.astype(bf16)
            p = _dot(pb, pb)
            tinv = tinv + _dot(tinv.astype(bf16), p.astype(bf16))

        av = _dot(jnp.concatenate([a_ak, a_rk], axis=0).astype(bf16), v_st.astype(bf16))
        wu = _dot(tinv.astype(bf16), jnp.concatenate([a_t, av[0:PAIR]], axis=1).astype(bf16))
        ry = jnp.concatenate([r_t, av[PAIR:]], axis=1) + _dot(a_rb.astype(bf16), wu.astype(bf16))
        ry = ry[0:C] + ry[C:]
        rm = ry[:, 0:PAIR]
        y0 = ry[:, PAIR:]

        st = st_ref[...]
        stb = st.astype(bf16)
        y = _dot(rm.astype(bf16), stb) + y0
        mct = jnp.where(eye, jnp.exp(LC), 0.0) + _dot_tn(b_h.astype(bf16), wu[:, 0:PAIR].astype(bf16))
        gct = _dot_tn(jnp.concatenate([b_h, k_h], axis=0).astype(bf16),
                      jnp.concatenate([wu[:, PAIR:], v_st], axis=0).astype(bf16))
        st_ref[...] = _dot(mct.astype(bf16), stb) + gct

        mu = _dot(y, head_avg, HI)
        d = y - mu
        var = _dot(d * d, head_avg, HI)
        yn = d * lax.rsqrt(var + GN_EPS) * lnw + lnb
        o_ref[rows, :] = ((yn + bon_s[rows, :]) * g_s[rows, :]).astype(o_ref.dtype)
        return carry

    lax.fori_loop(0, tr // C, chunk, 0)


def _rwkv(proj, B, S, mu_rkv, mu_lora, w0, a0, k_k, k_a, r_k, lnx_w, lnx_b, w_up_p, a_up_p, g_up):
    T = B * S
    tr = RWKV_TR
    nt = S // tr
    sub = tr // BF16_SUBLANES
    nsub = S // BF16_SUBLANES
    npair = RWKV_WIDTH // PAIR
    rcol, kcol, vcol = COL_R // PAIR, COL_KR // PAIR, COL_VR // PAIR
    locol = COL_LORA // LORA_PAD

    def cur(col):
        return pl.BlockSpec((tr, PAIR), lambda b, h, t: (b * nt + t, col + h))

    def prev(col):
        return pl.BlockSpec((BF16_SUBLANES, PAIR),
                            lambda b, h, t: (b * nsub + jnp.maximum(t * sub - 1, 0), col + h))

    def vec(off):
        return pl.BlockSpec((1, PAIR), lambda b, h, t: (0, off + h))

    in_specs = [
        cur(rcol), cur(kcol), cur(vcol),
        pl.BlockSpec((tr, LORA_PAD), lambda b, h, t: (b * nt + t, locol)),
        prev(rcol), prev(kcol), prev(vcol),
        pl.BlockSpec((BF16_SUBLANES, LORA_PAD),
                     lambda b, h, t: (b * nsub + jnp.maximum(t * sub - 1, 0), locol)),
        vec(0), vec(npair), vec(2 * npair),
        pl.BlockSpec((1, LORA_PAD), lambda b, h, t: (0, 0)),
        vec(0), vec(0), vec(0), vec(0), vec(0), vec(0), vec(0),
        pl.BlockSpec((LANES, PAIR), lambda b, h, t: (0, h)),
        pl.BlockSpec((LANES, PAIR), lambda b, h, t: (0, h)),
        pl.BlockSpec((GATE_LORA, PAIR), lambda b, h, t: (0, h)),
    ]
    scratch = [pltpu.VMEM((PAIR, PAIR), f32)] + [pltpu.VMEM((tr, PAIR), f32)] * 8
    return pl.pallas_call(
        _rwkv_kernel,
        out_shape=jax.ShapeDtypeStruct((T, RWKV_WIDTH), bf16),
        grid=(B, npair, nt),
        in_specs=in_specs,
        out_specs=pl.BlockSpec((tr, PAIR), lambda b, h, t: (b * nt + t, h)),
        scratch_shapes=scratch,
        compiler_params=pltpu.CompilerParams(
            dimension_semantics=("parallel", "parallel", "arbitrary"), vmem_limit_bytes=VMEM_LIMIT),
        name="rwkv7",
    )(proj, proj, proj, proj, proj, proj, proj, proj,
      mu_rkv, mu_rkv, mu_rkv, mu_lora, w0, a0, k_k, k_a, r_k, lnx_w, lnx_b, w_up_p, a_up_p, g_up)


def _layer_norm(z, w, b):
    mu = jnp.mean(z, axis=-1, keepdims=True)
    d = z - mu
    var = jnp.mean(d * d, axis=-1, keepdims=True)
    return d * lax.rsqrt(var + LN_EPS) * w + b


def _merge_kernel(x_ref, ya_ref, yb_ref, ga_ref, gb_ref, pa_ref, pb_ref, wo_ref, l1w_ref, l1b_ref,
                  wr_ref, br_ref,
                  h_ref, topi_ref, gate_ref, rank_ref, cnt_ref,
                  run_ref):
    i = pl.program_id(0)
    tm = x_ref.shape[0]
    ne = wr_ref.shape[1]

    @pl.when(i == 0)
    def _():
        run_ref[...] = jnp.zeros_like(run_ref)

    ma = _dot(ya_ref[...], pa_ref[...])
    mb = _dot(yb_ref[...], pb_ref[...])
    merged = (jax.nn.sigmoid(ga_ref[...].astype(f32)) * ma
              + jax.nn.sigmoid(gb_ref[...].astype(f32)) * mb)
    mix = _dot(merged.astype(bf16), wo_ref[...])
    h = _layer_norm(DEEPNORM_ALPHA * x_ref[...] + mix, l1w_ref[...], l1b_ref[...])
    h_ref[...] = h

    logits = _dot(h, wr_ref[...], HI) + br_ref[...]
    eidx = lax.broadcasted_iota(i32, (tm, ne), 1)
    slot = lax.broadcasted_iota(i32, (tm, LANES), 1)
    lg = logits
    vals, idxs = [], []
    for _ in range(TOP_K):
        m = jnp.max(lg, axis=-1, keepdims=True)
        idx = jnp.min(jnp.where(lg == m, eidx, ne), axis=-1, keepdims=True)
        vals.append(m)
        idxs.append(idx)
        lg = jnp.where(eidx == idx, -jnp.inf, lg)
    exps = [jnp.exp(vk - vals[0]) for vk in vals]
    denom = exps[0] + exps[1] + exps[2] + exps[3]

    onehots = [jnp.where(eidx == idx, 1.0, 0.0).astype(f32) for idx in idxs]
    oh = onehots[0] + onehots[1] + onehots[2] + onehots[3]
    tri = jnp.where(lax.broadcasted_iota(i32, (tm, tm), 1) < lax.broadcasted_iota(i32, (tm, tm), 0),
                    1.0, 0.0).astype(bf16)
    before = run_ref[...] + _dot(tri, oh.astype(bf16))

    topi_o = jnp.zeros((tm, LANES), i32)
    gate_o = jnp.zeros((tm, LANES), f32)
    rank_o = jnp.zeros((tm, LANES), i32)
    for kslot in range(TOP_K):
        rk = jnp.sum(onehots[kslot] * before, axis=-1, keepdims=True).astype(i32)
        topi_o = jnp.where(slot == kslot, idxs[kslot], topi_o)
        gate_o = jnp.where(slot == kslot, exps[kslot] / denom, gate_o)
        rank_o = jnp.where(slot == kslot, rk, rank_o)
    topi_ref[...] = topi_o
    gate_ref[...] = gate_o
    rank_ref[...] = rank_o
    run = run_ref[...] + jnp.sum(oh, axis=0, keepdims=True)
    run_ref[...] = run
    cnt_ref[...] = run


def _merge_router(x2, ya, yb, proj, pa, pb, wo, l1w, l1b, wr, br):
    T, D = x2.shape
    tm = MERGE_TM
    ne = wr.shape[1]
    gcol = COL_GATE // D

    def const(shape):
        return pl.BlockSpec(shape, lambda i: (0, 0), pipeline_mode=pl.Buffered(1))

    out_shape = (jax.ShapeDtypeStruct((T, D), f32),
                 jax.ShapeDtypeStruct((T, LANES), i32),
                 jax.ShapeDtypeStruct((T, LANES), f32),
                 jax.ShapeDtypeStruct((T, LANES), i32),
                 jax.ShapeDtypeStruct((1, ne), f32))
    return pl.pallas_call(
        _merge_kernel,
        out_shape=out_shape,
        grid=(T // tm,),
        in_specs=[pl.BlockSpec((tm, D), lambda i: (i, 0)),
                  pl.BlockSpec((tm, ATT_WIDTH), lambda i: (i, 0)),
                  pl.BlockSpec((tm, RWKV_WIDTH), lambda i: (i, 0)),
                  pl.BlockSpec((tm, D), lambda i: (i, gcol)),
                  pl.BlockSpec((tm, D), lambda i: (i, gcol + 1)),
                  const(pa.shape), const(pb.shape), const(wo.shape),
                  const((1, D)), const((1, D)), const(wr.shape), const((1, ne))],
        out_specs=(pl.BlockSpec((tm, D), lambda i: (i, 0)),
                   pl.BlockSpec((tm, LANES), lambda i: (i, 0)),
                   pl.BlockSpec((tm, LANES), lambda i: (i, 0)),
                   pl.BlockSpec((tm, LANES), lambda i: (i, 0)),
                   pl.BlockSpec((1, ne), lambda i: (0, 0))),
        scratch_shapes=[pltpu.VMEM((1, ne), f32)],
        compiler_params=pltpu.CompilerParams(
            dimension_semantics=("arbitrary",), vmem_limit_bytes=VMEM_LIMIT),
        name="merge_ln1_router",
    )(x2, ya, yb, proj, proj, pa, pb, wo, l1w, l1b, wr, br)


def _dispatch_kernel(dest_ref, h_hbm, xs_in_hbm, xs_hbm, sem):
    del xs_in_hbm
    n = dest_ref.shape[0]
    tm = n // TOP_K
    base = pl.program_id(0) * tm

    def row_copy(s):
        return pltpu.make_async_copy(h_hbm.at[pl.ds(base + s // TOP_K, 1)],
                                     xs_hbm.at[pl.ds(dest_ref[s], 1)], sem)

    def start(s, c):
        row_copy(s).start()
        return c

    def wait(s, c):
        row_copy(s).wait()
        return c

    lax.fori_loop(0, n, start, 0)
    lax.fori_loop(0, n, wait, 0)


def _dispatch(dest_flat, h, xs0):
    T, D = h.shape
    n = COMBINE_TM * TOP_K
    return pl.pallas_call(
        _dispatch_kernel,
        out_shape=jax.ShapeDtypeStruct(xs0.shape, xs0.dtype),
        grid=(T * TOP_K // n,),
        in_specs=[pl.BlockSpec((n,), lambda i: (i,), memory_space=pltpu.SMEM),
                  pl.BlockSpec(memory_space=pl.ANY),
                  pl.BlockSpec(memory_space=pl.ANY)],
        out_specs=pl.BlockSpec(memory_space=pl.ANY),
        scratch_shapes=[pltpu.SemaphoreType.DMA(())],
        input_output_aliases={2: 0},
        compiler_params=pltpu.CompilerParams(
            dimension_semantics=("arbitrary",), has_side_effects=True),
        name="moe_dispatch",
    )(dest_flat, h, xs0)


def _expert_kernel(be_ref, nu_ref, x_ref, wg_ref, wu_ref, bg_ref, bu_ref, wd_ref, bd_ref, o_ref, xb_ref):
    del be_ref
    i = pl.program_id(0)
    j = pl.program_id(1)
    used = i < nu_ref[0]

    @pl.when(jnp.logical_and(used, j == 0))
    def _():
        xb_ref[...] = x_ref[...].astype(bf16)
        o_ref[...] = jnp.broadcast_to(bd_ref[0], o_ref.shape)

    @pl.when(jnp.logical_and(jnp.logical_not(used), j == 0))
    def _():
        o_ref[...] = jnp.zeros_like(o_ref)

    @pl.when(used)
    def _():
        xb = xb_ref[...]
        g = _dot(xb, wg_ref[0]) + bg_ref[0]
        u = _dot(xb, wu_ref[0]) + bu_ref[0]
        g = jnp.minimum(g, SWIGLU_LIMIT)
        u = jnp.clip(u, -SWIGLU_LIMIT, SWIGLU_LIMIT)
        act = (u + 1.0) * (g * jax.nn.sigmoid(SWIGLU_ALPHA * g))
        o_ref[...] += _dot(act.astype(bf16), wd_ref[0])


def _experts(blk_e, nused, xs, w_gu, b_gu, w_down, b_down):
    R, D = xs.shape
    E, _, F2 = w_gu.shape
    F = F2 // 2
    rb, tf = MOE_RB, min(MOE_TF, F)
    nf = F // tf

    def jj(i, j, nu):
        return jnp.where(i < nu[0], j, 0)

    grid_spec = pltpu.PrefetchScalarGridSpec(
        num_scalar_prefetch=2,
        grid=(R // rb, nf),
        in_specs=[pl.BlockSpec((rb, D), lambda i, j, be, nu: (i, 0)),
                  pl.BlockSpec((1, D, tf), lambda i, j, be, nu: (be[i], 0, jj(i, j, nu))),
                  pl.BlockSpec((1, D, tf), lambda i, j, be, nu: (be[i], 0, nf + jj(i, j, nu))),
                  pl.BlockSpec((1, 1, tf), lambda i, j, be, nu: (be[i], 0, jj(i, j, nu))),
                  pl.BlockSpec((1, 1, tf), lambda i, j, be, nu: (be[i], 0, nf + jj(i, j, nu))),
                  pl.BlockSpec((1, tf, D), lambda i, j, be, nu: (be[i], jj(i, j, nu), 0)),
                  pl.BlockSpec((1, 1, D), lambda i, j, be, nu: (be[i], 0, 0))],
        out_specs=pl.BlockSpec((rb, D), lambda i, j, be, nu: (i, 0)),
        scratch_shapes=[pltpu.VMEM((rb, D), bf16)],
    )
    return pl.pallas_call(
        _expert_kernel,
        out_shape=jax.ShapeDtypeStruct((R, D), f32),
        grid_spec=grid_spec,
        compiler_params=pltpu.CompilerParams(
            dimension_semantics=("parallel", "arbitrary"), vmem_limit_bytes=VMEM_LIMIT),
        name="moe_experts",
    )(blk_e, nused, xs, w_gu, w_gu, b_gu, b_gu, w_down, b_down)


def _combine_kernel(dest_ref, h_ref, gate_ref, l2w_ref, l2b_ref, y_hbm, o_ref, buf, sem):
    n = dest_ref.shape[0]

    def row_copy(s):
        return pltpu.make_async_copy(y_hbm.at[pl.ds(dest_ref[s], 1)],
                                     buf.at[s % TOP_K, pl.ds(s // TOP_K, 1)], sem)

    def start(s, c):
        row_copy(s).start()
        return c

    def wait(s, c):
        row_copy(s).wait()
        return c

    lax.fori_loop(0, n, start, 0)
    lax.fori_loop(0, n, wait, 0)

    gate = gate_ref[...]
    ffn = gate[:, 0:1] * buf[0]
    for kslot in range(1, TOP_K):
        ffn = ffn + gate[:, kslot:kslot + 1] * buf[kslot]
    o_ref[...] = _layer_norm(DEEPNORM_ALPHA * h_ref[...] + ffn, l2w_ref[...], l2b_ref[...])


def _combine(dest_flat, h, gate, l2w, l2b, y):
    T, D = h.shape
    tm = COMBINE_TM
    n = tm * TOP_K
    return pl.pallas_call(
        _combine_kernel,
        out_shape=jax.ShapeDtypeStruct((T, D), f32),
        grid=(T // tm,),
        in_specs=[pl.BlockSpec((n,), lambda i: (i,), memory_space=pltpu.SMEM),
                  pl.BlockSpec((tm, D), lambda i: (i, 0)),
                  pl.BlockSpec((tm, LANES), lambda i: (i, 0)),
                  pl.BlockSpec((1, D), lambda i: (0, 0)),
                  pl.BlockSpec((1, D), lambda i: (0, 0)),
                  pl.BlockSpec(memory_space=pl.ANY)],
        out_specs=pl.BlockSpec((tm, D), lambda i: (i, 0)),
        scratch_shapes=[pltpu.VMEM((TOP_K, tm, D), f32), pltpu.SemaphoreType.DMA(())],
        compiler_params=pltpu.CompilerParams(
            dimension_semantics=("arbitrary",), vmem_limit_bytes=VMEM_LIMIT),
        name="moe_combine_ln2",
    )(dest_flat, h, gate, l2w, l2b, y)


def _pack_w_in(w_in):
    o1 = 3 * ATT_WIDTH
    o2 = o1 + 3 * RWKV_WIDTH
    o3 = o2 + DECAY_LORA
    o4 = o3 + ICLR_LORA
    o5 = o4 + GATE_LORA
    padc = lambda w, n: jnp.pad(w, ((0, 0), (0, n - w.shape[1])))
    parts = [w_in[:, o5:], w_in[:, :o2], padc(w_in[:, o2:o3], LANES), padc(w_in[:, o3:o4], LANES),
             w_in[:, o4:o5]]
    return jnp.concatenate(parts, axis=1).astype(bf16)


def _token_mixing(x2, B, S, w_in, rel_bias, shift_mu, w0, w_up, a0, a_up, g_up, k_k, k_a, r_k,
                  lnx_w, lnx_b):
    T, D = x2.shape
    wp = _pack_w_in(w_in)
    proj = _in_proj(x2, wp, min(1024, T), 768)

    y_a = _attention(proj, _attn_bias_table(rel_bias), B, S)

    W = RWKV_WIDTH
    row = lambda p: p.reshape(1, -1).astype(f32)
    padv = lambda p, n: jnp.pad(p, (0, n - p.shape[0]))
    mu_rkv = row(shift_mu[:3 * W])
    mu_lora = row(jnp.concatenate([padv(shift_mu[3 * W:3 * W + DECAY_LORA], LANES),
                                   padv(shift_mu[3 * W + DECAY_LORA:3 * W + DECAY_LORA + ICLR_LORA], LANES),
                                   shift_mu[3 * W + DECAY_LORA + ICLR_LORA:]]))
    padr = lambda w: jnp.pad(w, ((0, LANES - w.shape[0]), (0, 0))).astype(bf16)
    y_b = _rwkv(proj, B, S, mu_rkv, mu_lora, row(w0), row(a0), row(k_k), row(k_a), row(r_k),
                row(lnx_w), row(lnx_b), padr(w_up), padr(a_up), g_up.astype(bf16))
    return proj, y_a, y_b


def _moe_routing(topi, rank, counts, n_experts, rb):
    T = topi.shape[0]
    M = T * TOP_K
    counts = counts.astype(i32)
    padded = (counts + rb - 1) // rb * rb
    pad_ends = jnp.cumsum(padded)
    pad_starts = pad_ends - padded
    dest = (pad_starts[topi] + rank).reshape(M)
    n_blocks = (M + n_experts * (rb - 1) + rb - 1) // rb
    blk_e = jnp.minimum(jnp.searchsorted(pad_ends, jnp.arange(n_blocks, dtype=i32) * rb, side='right'),
                        n_experts - 1).astype(i32)
    nused = (pad_ends[-1:] // rb).astype(i32)
    return dest.astype(i32), blk_e, nused, n_blocks


def kernel(x, w_in, rel_bias, shift_mu, w0, w_up, a0, a_up, g_up, k_k, k_a, r_k, lnx_w, lnx_b,
           proj_a, proj_b, w_out, ln1_w, ln1_b, w_router, b_router, w_gu, b_gu, w_down, b_down,
           ln2_w, ln2_b):
    B, S, D = x.shape
    T = B * S
    h = x.reshape(T, D)
    for l in range(DEPTH):
        proj, y_a, y_b = _token_mixing(h, B, S, w_in[l], rel_bias[l], shift_mu[l], w0[l], w_up[l],
                                       a0[l], a_up[l], g_up[l], k_k[l], k_a[l], r_k[l],
                                       lnx_w[l], lnx_b[l])
        E = w_router.shape[-1]
        h1, topi, gate, rank, counts = _merge_router(
            h, y_a, y_b, proj, proj_a[l].astype(bf16), proj_b[l].astype(bf16), w_out[l].astype(bf16),
            ln1_w[l].reshape(1, D), ln1_b[l].reshape(1, D), w_router[l], b_router[l].reshape(1, E))
        dest, blk_e, nused, n_blocks = _moe_routing(topi[:, :TOP_K], rank[:, :TOP_K], counts[0], E, MOE_RB)
        xs = _dispatch(dest, h1, jnp.zeros((n_blocks * MOE_RB, D), f32))
        y = _experts(blk_e, nused, xs, w_gu[l].astype(bf16), b_gu[l].reshape(E, 1, -1),
                     w_down[l].astype(bf16), b_down[l].reshape(E, 1, D))
        h = _combine(dest, h1, gate, ln2_w[l].reshape(1, D), ln2_b[l].reshape(1, D), y)
    return h.reshape(B, S, D)
```

```python
import functools

import jax
import jax.numpy as jnp
from jax import lax
from jax.experimental import pallas as pl
from jax.experimental.pallas import tpu as pltpu

f32 = jnp.float32
bf16 = jnp.bfloat16
i32 = jnp.int32

CHUNK = 64
LEFT_CHUNKS = 8
HEAD_DIM = 64
ATT_WIDTH = 1024
MAX_PAST_DIST = 256
REL_TABLE = MAX_PAST_DIST + CHUNK
RWKV_WIDTH = 1024
DECAY_LORA = 96
ICLR_LORA = 96
GATE_LORA = 256
TOP_K = 4
SWIGLU_LIMIT = 7.0
SWIGLU_ALPHA = 1.702
LN_EPS = 1e-5
GN_EPS = 64e-5
DEPTH = 1
DEEPNORM_ALPHA = (2 * DEPTH) ** 0.25

LANES = 128
PAIR = 2 * HEAD_DIM
BF16_SUBLANES = 16
VMEM_LIMIT = 56 * 1024 * 1024

NEG = -1e30
HI = lax.Precision.HIGHEST

LORA_PAD = 512
COL_GATE = 0
COL_Q = 4096
COL_K = COL_Q + ATT_WIDTH
COL_V = COL_K + ATT_WIDTH
COL_R = COL_V + ATT_WIDTH
COL_KR = COL_R + RWKV_WIDTH
COL_VR = COL_KR + RWKV_WIDTH
COL_LORA = COL_VR + RWKV_WIDTH
PROJ_COLS = COL_LORA + LORA_PAD

ATT_TQ = 256
RWKV_TR = 512
MERGE_TM = 256
MOE_RB = 512
MOE_TF = 512
COMBINE_TM = 256


def _dot(a, b, precision=None):
    return jnp.dot(a, b, preferred_element_type=f32, precision=precision)


def _dot_nt(a, b):
    return lax.dot_general(a, b, (((1,), (1,)), ((), ())), preferred_element_type=f32)


def _dot_tn(a, b):
    return lax.dot_general(a, b, (((0,), (0,)), ((), ())), preferred_element_type=f32)


def _split_bf16(a):
    hi = a.astype(bf16)
    return hi, (a - hi.astype(f32)).astype(bf16)


def _dot_split_lhs(a, b_bf16):
    hi, lo = _split_bf16(a)
    return _dot(hi, b_bf16) + _dot(lo, b_bf16)


def _dot_split_rhs(a_bf16, b):
    hi, lo = _split_bf16(b)
    return _dot(a_bf16, hi) + _dot(a_bf16, lo)


def _inproj_kernel(x_ref, w_ref, o_ref, xb_ref):
    @pl.when(pl.program_id(1) == 0)
    def _():
        xb_ref[...] = x_ref[...].astype(bf16)

    o_ref[...] = _dot(xb_ref[...], w_ref[...]).astype(o_ref.dtype)


def _in_proj(x2, w, tm, tn):
    M, K = x2.shape
    N = w.shape[1]
    return pl.pallas_call(
        _inproj_kernel,
        out_shape=jax.ShapeDtypeStruct((M, N), bf16),
        grid=(M // tm, N // tn),
        in_specs=[pl.BlockSpec((tm, K), lambda i, j: (i, 0)),
                  pl.BlockSpec((K, tn), lambda i, j: (0, j))],
        out_specs=pl.BlockSpec((tm, tn), lambda i, j: (i, j)),
        scratch_shapes=[pltpu.VMEM((tm, K), bf16)],
        compiler_params=pltpu.CompilerParams(
            dimension_semantics=("parallel", "arbitrary"), vmem_limit_bytes=VMEM_LIMIT),
        name="in_proj",
    )(x2, w)


def _attn_bias_table(rel_bias):
    tq = ATT_TQ
    nk = 3 * tq
    H = rel_bias.shape[0]
    span = nk + tq
    dist = jnp.arange(span - 1) - (tq - 1)
    idx = jnp.clip(jnp.minimum(dist, MAX_PAST_DIST) + (CHUNK - 1), 0, REL_TABLE - 1)
    u = jnp.pad(rel_bias.astype(f32)[:, idx], ((0, 0), (0, 1)))
    G = jnp.broadcast_to(u[:, None, :], (H, nk, span)).reshape(H, nk * span)
    G = G[:, :nk * (span - 1)].reshape(H, nk, span - 1)
    b = jnp.transpose(G[:, :, nk - 1:], (0, 2, 1))
    qi = jnp.arange(tq)[:, None]
    kj = jnp.arange(nk)[None, :]
    dchunk = (qi + 2 * tq) // CHUNK - kj // CHUNK
    ok = (dchunk >= 0) & (dchunk <= LEFT_CHUNKS)
    return jnp.where(ok[None], b, NEG)


def _attn_kernel(q_ref, k0_ref, k1_ref, k2_ref, v0_ref, v1_ref, v2_ref, bias_ref, o_ref):
    qt = pl.program_id(1)
    tq = q_ref.shape[0]
    k_refs = (k0_ref, k1_ref, k2_ref)
    v_refs = (v0_ref, v1_ref, v2_ref)
    outs = []
    for h in range(2):
        sl = slice(h * HEAD_DIM, (h + 1) * HEAD_DIM)
        q = q_ref[:, sl] * (HEAD_DIM ** -0.5)
        parts = []
        for j in range(3):
            s = _dot_nt(q, k_refs[j][:, sl]) + bias_ref[h, :, j * tq:(j + 1) * tq]
            if j < 2:
                s = jnp.where(qt - 2 + j >= 0, s, NEG)
            parts.append(s)
        m = jnp.maximum(jnp.maximum(jnp.max(parts[0], axis=-1, keepdims=True),
                                    jnp.max(parts[1], axis=-1, keepdims=True)),
                        jnp.max(parts[2], axis=-1, keepdims=True))
        acc = jnp.zeros((tq, HEAD_DIM), f32)
        l = jnp.zeros((tq, 1), f32)
        for j in range(3):
            p = jnp.exp(parts[j] - m)
            l = l + jnp.sum(p, axis=-1, keepdims=True)
            acc = acc + _dot(p.astype(bf16), v_refs[j][:, sl])
        outs.append(acc / l)
    o_ref[...] = jnp.concatenate(outs, axis=1).astype(o_ref.dtype)


def _attention(proj, bias, B, S):
    T = B * S
    tq = ATT_TQ
    nt = S // tq
    qcol, kcol, vcol = COL_Q // PAIR, COL_K // PAIR, COL_V // PAIR

    def kv_spec(col, back):
        return pl.BlockSpec((tq, PAIR),
                            lambda b, t, h: (b * nt + jnp.maximum(t - back, 0), col + h))

    return pl.pallas_call(
        _attn_kernel,
        out_shape=jax.ShapeDtypeStruct((T, ATT_WIDTH), bf16),
        grid=(B, nt, ATT_WIDTH // PAIR),
        in_specs=[pl.BlockSpec((tq, PAIR), lambda b, t, h: (b * nt + t, qcol + h)),
                  kv_spec(kcol, 2), kv_spec(kcol, 1), kv_spec(kcol, 0),
                  kv_spec(vcol, 2), kv_spec(vcol, 1), kv_spec(vcol, 0),
                  pl.BlockSpec((2, tq, 3 * tq), lambda b, t, h: (h, 0, 0))],
        out_specs=pl.BlockSpec((tq, PAIR), lambda b, t, h: (b * nt + t, h)),
        compiler_params=pltpu.CompilerParams(
            dimension_semantics=("parallel", "parallel", "parallel"), vmem_limit_bytes=VMEM_LIMIT),
        name="chunk_attn",
    )(proj, proj, proj, proj, proj, proj, proj, bias)


def _rwkv_kernel(r_ref, k_ref, v_ref, lo_ref, rp_ref, kp_ref, vp_ref, lop_ref,
                 mur_ref, muk_ref, muv_ref, mulo_ref, w0_ref, a0_ref, kkp_ref, kap_ref, rkp_ref,
                 lnw_ref, lnb_ref, wup_ref, aup_ref, gup_ref,
                 o_ref,
                 st_ref, r_s, lw_s, k_s, v_s, a_s, b_s, g_s, bon_s):
    t = pl.program_id(2)
    tr = r_ref.shape[0]
    C = CHUNK

    @pl.when(t == 0)
    def _():
        st_ref[...] = jnp.zeros_like(st_ref)

    not_first = jnp.where(t == 0, 0.0, 1.0).astype(f32)
    row = lax.broadcasted_iota(i32, (tr, 1), 0)

    def shift(p_ref, prev_ref, mu_ref):
        p = p_ref[...].astype(f32)
        prev_row = prev_ref[BF16_SUBLANES - 1:BF16_SUBLANES, :].astype(f32) * not_first
        prev = jnp.where(row == 0, prev_row, pltpu.roll(p, 1, 0))
        return p + (prev - p) * mu_ref[...]

    r = shift(r_ref, rp_ref, mur_ref)
    k = shift(k_ref, kp_ref, muk_ref)
    v = shift(v_ref, vp_ref, muv_ref)
    lo = shift(lo_ref, lop_ref, mulo_ref)
    wd = lo[:, 0:LANES]
    ad = lo[:, LANES:2 * LANES]
    gd = lo[:, 2 * LANES:]

    ri = lax.broadcasted_iota(i32, (PAIR, PAIR), 0)
    ci = lax.broadcasted_iota(i32, (PAIR, PAIR), 1)
    same_head = (ri // HEAD_DIM) == (ci // HEAD_DIM)
    head_ones = jnp.where(same_head, 1.0, 0.0).astype(bf16)
    head_avg = jnp.where(same_head, 1.0 / HEAD_DIM, 0.0).astype(bf16)

    z = w0_ref[...] + _dot(jnp.tanh(wd).astype(bf16), wup_ref[...])
    softplus_negz = jnp.maximum(-z, 0.0) + jnp.log(1.0 + jnp.exp(-jnp.abs(z)))
    lw = -jnp.exp(-softplus_negz - 0.5)
    alr = jax.nn.sigmoid(a0_ref[...] + _dot(ad.astype(bf16), aup_ref[...]))
    g = _dot(jax.nn.sigmoid(gd).astype(bf16), gup_ref[...])
    kk = k * kkp_ref[...]
    kk = kk / jnp.maximum(jnp.sqrt(_dot_split_lhs(kk * kk, head_ones)), 1e-12)
    k2 = k * (1.0 + (alr - 1.0) * kap_ref[...])
    bonus = _dot_split_lhs(r * k2 * rkp_ref[...], head_ones) * v

    r_s[...] = r
    lw_s[...] = lw
    k_s[...] = k2
    v_s[...] = v
    a_s[...] = -kk
    b_s[...] = kk * alr
    g_s[...] = g
    bon_s[...] = bonus

    lane = lax.broadcasted_iota(i32, (1, PAIR), 1)
    head0 = lane < HEAD_DIM
    tri_incl = jnp.where(lax.broadcasted_iota(i32, (C, C), 1) <= lax.broadcasted_iota(i32, (C, C), 0),
                         1.0, 0.0).astype(bf16)
    strict = ci < ri
    incl = ci <= ri
    eye = ci == ri
    lnw = lnw_ref[...]
    lnb = lnb_ref[...]

    def stack(x):
        return jnp.concatenate([jnp.where(head0, x, 0.0), jnp.where(head0, 0.0, x)], axis=0)

    nc = tr // C
    chunks = range(nc)
    rows = [slice(c * C, (c + 1) * C) for c in chunks]
    lwc = [lw_s[rw, :] for rw in rows]
    L = [_dot_split_rhs(tri_incl, x) for x in lwc]
    LC = [x[C - 1:C, :] for x in L]
    a_t, r_t, b_t, k_t, b_h, k_h, v_st = [], [], [], [], [], [], []
    for c in chunks:
        rw = rows[c]
        e_pos = jnp.exp(L[c])
        e_neg = jnp.exp(-L[c])
        e_prev = jnp.exp(L[c] - lwc[c])
        e_end = jnp.exp(LC[c] - L[c])
        bc = b_s[rw, :]
        kc = k_s[rw, :]
        a_t.append(stack(a_s[rw, :] * e_prev))
        r_t.append(stack(r_s[rw, :] * e_pos))
        b_t.append(stack(bc * e_neg))
        k_t.append(stack(kc * e_neg))
        b_h.append(stack(bc * e_end))
        k_h.append(stack(kc * e_end))
        v_st.append(stack(v_s[rw, :]))

    A = [_dot_nt(jnp.concatenate([a_t[c], r_t[c]], axis=0).astype(bf16),
                 jnp.concatenate([b_t[c], k_t[c]], axis=0).astype(bf16)) for c in chunks]
    a_ab = [jnp.where(strict, x[0:PAIR, 0:PAIR], 0.0) for x in A]
    a_ak = [jnp.where(strict, x[0:PAIR, PAIR:], 0.0) for x in A]
    a_rb = [jnp.where(incl, x[PAIR:, 0:PAIR], 0.0) for x in A]
    a_rk = [jnp.where(incl, x[PAIR:, PAIR:], 0.0) for x in A]

    p = a_ab
    tinv = [jnp.where(eye, 1.0, 0.0).astype(f32) + x for x in a_ab]
    for _ in range(5):
        pb = [x.astype(bf16) for x in p]
        p = [_dot(x, x) for x in pb]
        tinv = [tv + _dot(tv.astype(bf16), x.astype(bf16)) for tv, x in zip(tinv, p)]

    av = [_dot(jnp.concatenate([a_ak[c], a_rk[c]], axis=0).astype(bf16), v_st[c].astype(bf16))
          for c in chunks]
    wu = [_dot(tinv[c].astype(bf16), jnp.concatenate([a_t[c], av[c][0:PAIR]], axis=1).astype(bf16))
          for c in chunks]
    ry = [jnp.concatenate([r_t[c], av[c][PAIR:]], axis=1) + _dot(a_rb[c].astype(bf16), wu[c].astype(bf16))
          for c in chunks]
    ry = [x[0:C] + x[C:] for x in ry]
    mct = [jnp.where(eye, jnp.exp(LC[c]), 0.0) + _dot_tn(b_h[c].astype(bf16), wu[c][:, 0:PAIR].astype(bf16))
           for c in chunks]
    gct = [_dot_tn(jnp.concatenate([b_h[c], k_h[c]], axis=0).astype(bf16),
                   jnp.concatenate([wu[c][:, PAIR:], v_st[c]], axis=0).astype(bf16)) for c in chunks]

    st = st_ref[...]
    y = []
    for c in chunks:
        stb = st.astype(bf16)
        y.append(_dot(ry[c][:, 0:PAIR].astype(bf16), stb) + ry[c][:, PAIR:])
        st = _dot(mct[c].astype(bf16), stb) + gct[c]
    st_ref[...] = st

    mu = [_dot_split_lhs(x, head_avg) for x in y]
    d = [x - m for x, m in zip(y, mu)]
    var = [_dot_split_lhs(x * x, head_avg) for x in d]
    for c in chunks:
        yn = d[c] * lax.rsqrt(var[c] + GN_EPS) * lnw + lnb
        o_ref[rows[c], :] = ((yn + bon_s[rows[c], :]) * g_s[rows[c], :]).astype(o_ref.dtype)


def _rwkv(proj, B, S, mu_rkv, mu_lora, w0, a0, k_k, k_a, r_k, lnx_w, lnx_b, w_up_p, a_up_p, g_up):
    T = B * S
    tr = RWKV_TR
    nt = S // tr
    sub = tr // BF16_SUBLANES
    nsub = S // BF16_SUBLANES
    npair = RWKV_WIDTH // PAIR
    rcol, kcol, vcol = COL_R // PAIR, COL_KR // PAIR, COL_VR // PAIR
    locol = COL_LORA // LORA_PAD

    def cur(col):
        return pl.BlockSpec((tr, PAIR), lambda b, h, t: (b * nt + t, col + h))

    def prev(col):
        return pl.BlockSpec((BF16_SUBLANES, PAIR),
                            lambda b, h, t: (b * nsub + jnp.maximum(t * sub - 1, 0), col + h))

    def vec(off):
        return pl.BlockSpec((1, PAIR), lambda b, h, t: (0, off + h))

    in_specs = [
        cur(rcol), cur(kcol), cur(vcol),
        pl.BlockSpec((tr, LORA_PAD), lambda b, h, t: (b * nt + t, locol)),
        prev(rcol), prev(kcol), prev(vcol),
        pl.BlockSpec((BF16_SUBLANES, LORA_PAD),
                     lambda b, h, t: (b * nsub + jnp.maximum(t * sub - 1, 0), locol)),
        vec(0), vec(npair), vec(2 * npair),
        pl.BlockSpec((1, LORA_PAD), lambda b, h, t: (0, 0)),
        vec(0), vec(0), vec(0), vec(0), vec(0), vec(0), vec(0),
        pl.BlockSpec((LANES, PAIR), lambda b, h, t: (0, h)),
        pl.BlockSpec((LANES, PAIR), lambda b, h, t: (0, h)),
        pl.BlockSpec((GATE_LORA, PAIR), lambda b, h, t: (0, h)),
    ]
    scratch = [pltpu.VMEM((PAIR, PAIR), f32)] + [pltpu.VMEM((tr, PAIR), f32)] * 8
    return pl.pallas_call(
        _rwkv_kernel,
        out_shape=jax.ShapeDtypeStruct((T, RWKV_WIDTH), bf16),
        grid=(B, npair, nt),
        in_specs=in_specs,
        out_specs=pl.BlockSpec((tr, PAIR), lambda b, h, t: (b * nt + t, h)),
        scratch_shapes=scratch,
        compiler_params=pltpu.CompilerParams(
            dimension_semantics=("parallel", "parallel", "arbitrary"), vmem_limit_bytes=VMEM_LIMIT),
        name="rwkv7",
    )(proj, proj, proj, proj, proj, proj, proj, proj,
      mu_rkv, mu_rkv, mu_rkv, mu_lora, w0, a0, k_k, k_a, r_k, lnx_w, lnx_b, w_up_p, a_up_p, g_up)


def _layer_norm(z, w, b):
    mu = jnp.mean(z, axis=-1, keepdims=True)
    d = z - mu
    var = jnp.mean(d * d, axis=-1, keepdims=True)
    return d * lax.rsqrt(var + LN_EPS) * w + b


def _merge_kernel(x_ref, ya_ref, yb_ref, ga_ref, gb_ref, pa_ref, pb_ref, wo_ref, l1w_ref, l1b_ref,
                  wr_ref, br_ref,
                  h_ref, topi_ref, gate_ref, rank_ref, cnt_ref,
                  run_ref):
    i = pl.program_id(0)
    tm = x_ref.shape[0]
    ne = wr_ref.shape[1]

    @pl.when(i == 0)
    def _():
        run_ref[...] = jnp.zeros_like(run_ref)

    ma = _dot(ya_ref[...], pa_ref[...])
    mb = _dot(yb_ref[...], pb_ref[...])
    merged = (jax.nn.sigmoid(ga_ref[...].astype(f32)) * ma
              + jax.nn.sigmoid(gb_ref[...].astype(f32)) * mb)
    mix = _dot(merged.astype(bf16), wo_ref[...])
    h = _layer_norm(DEEPNORM_ALPHA * x_ref[...] + mix, l1w_ref[...], l1b_ref[...])
    h_ref[...] = h

    logits = _dot(h, wr_ref[...], HI) + br_ref[...]
    eidx = lax.broadcasted_iota(i32, (tm, ne), 1)
    slot = lax.broadcasted_iota(i32, (tm, LANES), 1)
    lg = logits
    vals, idxs = [], []
    for _ in range(TOP_K):
        m = jnp.max(lg, axis=-1, keepdims=True)
        idx = jnp.min(jnp.where(lg == m, eidx, ne), axis=-1, keepdims=True)
        vals.append(m)
        idxs.append(idx)
        lg = jnp.where(eidx == idx, -jnp.inf, lg)
    exps = [jnp.exp(vk - vals[0]) for vk in vals]
    denom = exps[0] + exps[1] + exps[2] + exps[3]

    onehots = [jnp.where(eidx == idx, 1.0, 0.0).astype(f32) for idx in idxs]
    oh = onehots[0] + onehots[1] + onehots[2] + onehots[3]
    tri = jnp.where(lax.broadcasted_iota(i32, (tm, tm), 1) < lax.broadcasted_iota(i32, (tm, tm), 0),
                    1.0, 0.0).astype(bf16)
    before = run_ref[...] + _dot(tri, oh.astype(bf16))

    topi_o = jnp.zeros((tm, LANES), i32)
    gate_o = jnp.zeros((tm, LANES), f32)
    rank_o = jnp.zeros((tm, LANES), i32)
    for kslot in range(TOP_K):
        rk = jnp.sum(onehots[kslot] * before, axis=-1, keepdims=True).astype(i32)
        topi_o = jnp.where(slot == kslot, idxs[kslot], topi_o)
        gate_o = jnp.where(slot == kslot, exps[kslot] / denom, gate_o)
        rank_o = jnp.where(slot == kslot, rk, rank_o)
    topi_ref[...] = topi_o
    gate_ref[...] = gate_o
    rank_ref[...] = rank_o
    run = run_ref[...] + jnp.sum(oh, axis=0, keepdims=True)
    run_ref[...] = run
    cnt_ref[...] = run


def _merge_router(x2, ya, yb, proj, pa, pb, wo, l1w, l1b, wr, br):
    T, D = x2.shape
    tm = MERGE_TM
    ne = wr.shape[1]
    gcol = COL_GATE // D

    def const(shape):
        return pl.BlockSpec(shape, lambda i: (0, 0), pipeline_mode=pl.Buffered(1))

    out_shape = (jax.ShapeDtypeStruct((T, D), f32),
                 jax.ShapeDtypeStruct((T, LANES), i32),
                 jax.ShapeDtypeStruct((T, LANES), f32),
                 jax.ShapeDtypeStruct((T, LANES), i32),
                 jax.ShapeDtypeStruct((1, ne), f32))
    return pl.pallas_call(
        _merge_kernel,
        out_shape=out_shape,
        grid=(T // tm,),
        in_specs=[pl.BlockSpec((tm, D), lambda i: (i, 0)),
                  pl.BlockSpec((tm, ATT_WIDTH), lambda i: (i, 0)),
                  pl.BlockSpec((tm, RWKV_WIDTH), lambda i: (i, 0)),
                  pl.BlockSpec((tm, D), lambda i: (i, gcol)),
                  pl.BlockSpec((tm, D), lambda i: (i, gcol + 1)),
                  const(pa.shape), const(pb.shape), const(wo.shape),
                  const((1, D)), const((1, D)), const(wr.shape), const((1, ne))],
        out_specs=(pl.BlockSpec((tm, D), lambda i: (i, 0)),
                   pl.BlockSpec((tm, LANES), lambda i: (i, 0)),
                   pl.BlockSpec((tm, LANES), lambda i: (i, 0)),
                   pl.BlockSpec((tm, LANES), lambda i: (i, 0)),
                   pl.BlockSpec((1, ne), lambda i: (0, 0))),
        scratch_shapes=[pltpu.VMEM((1, ne), f32)],
        compiler_params=pltpu.CompilerParams(
            dimension_semantics=("arbitrary",), vmem_limit_bytes=VMEM_LIMIT),
        name="merge_ln1_router",
    )(x2, ya, yb, proj, proj, pa, pb, wo, l1w, l1b, wr, br)


def _dispatch_kernel(dest_ref, h_ref, xs_in_hbm, xs_hbm, sem):
    del xs_in_hbm
    n = dest_ref.shape[0]

    def row_copy(s):
        return pltpu.make_async_copy(h_ref.at[pl.ds(s // TOP_K, 1)],
                                     xs_hbm.at[pl.ds(dest_ref[s], 1)], sem)

    def start(s, c):
        row_copy(s).start()
        return c

    def wait(s, c):
        row_copy(s).wait()
        return c

    lax.fori_loop(0, n, start, 0)
    lax.fori_loop(0, n, wait, 0)


def _dispatch(dest_flat, h, xs0):
    T, D = h.shape
    n = COMBINE_TM * TOP_K
    return pl.pallas_call(
        _dispatch_kernel,
        out_shape=jax.ShapeDtypeStruct(xs0.shape, xs0.dtype),
        grid=(T * TOP_K // n,),
        in_specs=[pl.BlockSpec((n,), lambda i: (i,), memory_space=pltpu.SMEM),
                  pl.BlockSpec((n // TOP_K, D), lambda i: (i, 0)),
                  pl.BlockSpec(memory_space=pl.ANY)],
        out_specs=pl.BlockSpec(memory_space=pl.ANY),
        scratch_shapes=[pltpu.SemaphoreType.DMA(())],
        input_output_aliases={2: 0},
        compiler_params=pltpu.CompilerParams(
            dimension_semantics=("arbitrary",), has_side_effects=True, vmem_limit_bytes=VMEM_LIMIT),
        name="moe_dispatch",
    )(dest_flat, h, xs0)


def _expert_kernel(be_ref, nu_ref, x_ref, wg_ref, wu_ref, bg_ref, bu_ref, wd_ref, bd_ref, o_ref, xb_ref):
    del be_ref
    i = pl.program_id(0)
    j = pl.program_id(1)
    used = i < nu_ref[0]

    @pl.when(jnp.logical_and(used, j == 0))
    def _():
        xb_ref[...] = x_ref[...].astype(bf16)
        o_ref[...] = jnp.broadcast_to(bd_ref[0], o_ref.shape)

    @pl.when(jnp.logical_and(jnp.logical_not(used), j == 0))
    def _():
        o_ref[...] = jnp.zeros_like(o_ref)

    @pl.when(used)
    def _():
        xb = xb_ref[...]
        g = _dot(xb, wg_ref[0]) + bg_ref[0]
        u = _dot(xb, wu_ref[0]) + bu_ref[0]
        g = jnp.minimum(g, SWIGLU_LIMIT)
        u = jnp.clip(u, -SWIGLU_LIMIT, SWIGLU_LIMIT)
        act = (u + 1.0) * (g * jax.nn.sigmoid(SWIGLU_ALPHA * g))
        o_ref[...] += _dot(act.astype(bf16), wd_ref[0])


def _experts(blk_e, nused, xs, w_gu, b_gu, w_down, b_down):
    R, D = xs.shape
    E, _, F2 = w_gu.shape
    F = F2 // 2
    rb, tf = MOE_RB, min(MOE_TF, F)
    nf = F // tf

    def jj(i, j, nu):
        return jnp.where(i < nu[0], j, 0)

    grid_spec = pltpu.PrefetchScalarGridSpec(
        num_scalar_prefetch=2,
        grid=(R // rb, nf),
        in_specs=[pl.BlockSpec((rb, D), lambda i, j, be, nu: (i, 0)),
                  pl.BlockSpec((1, D, tf), lambda i, j, be, nu: (be[i], 0, jj(i, j, nu))),
                  pl.BlockSpec((1, D, tf), lambda i, j, be, nu: (be[i], 0, nf + jj(i, j, nu))),
                  pl.BlockSpec((1, 1, tf), lambda i, j, be, nu: (be[i], 0, jj(i, j, nu))),
                  pl.BlockSpec((1, 1, tf), lambda i, j, be, nu: (be[i], 0, nf + jj(i, j, nu))),
                  pl.BlockSpec((1, tf, D), lambda i, j, be, nu: (be[i], jj(i, j, nu), 0)),
                  pl.BlockSpec((1, 1, D), lambda i, j, be, nu: (be[i], 0, 0))],
        out_specs=pl.BlockSpec((rb, D), lambda i, j, be, nu: (i, 0)),
        scratch_shapes=[pltpu.VMEM((rb, D), bf16)],
    )
    return pl.pallas_call(
        _expert_kernel,
        out_shape=jax.ShapeDtypeStruct((R, D), f32),
        grid_spec=grid_spec,
        compiler_params=pltpu.CompilerParams(
            dimension_semantics=("parallel", "arbitrary"), vmem_limit_bytes=VMEM_LIMIT),
        name="moe_experts",
    )(blk_e, nused, xs, w_gu, w_gu, b_gu, b_gu, w_down, b_down)


def _combine_kernel(dest_ref, h_ref, gate_ref, l2w_ref, l2b_ref, y_hbm, o_ref, buf, sem):
    n = dest_ref.shape[0]

    def row_copy(s):
        return pltpu.make_async_copy(y_hbm.at[pl.ds(dest_ref[s], 1)],
                                     buf.at[s % TOP_K, pl.ds(s // TOP_K, 1)], sem)

    def start(s, c):
        row_copy(s).start()
        return c

    def wait(s, c):
        row_copy(s).wait()
        return c

    lax.fori_loop(0, n, start, 0)
    lax.fori_loop(0, n, wait, 0)

    gate = gate_ref[...]
    ffn = gate[:, 0:1] * buf[0]
    for kslot in range(1, TOP_K):
        ffn = ffn + gate[:, kslot:kslot + 1] * buf[kslot]
    o_ref[...] = _layer_norm(DEEPNORM_ALPHA * h_ref[...] + ffn, l2w_ref[...], l2b_ref[...])


def _combine(dest_flat, h, gate, l2w, l2b, y):
    T, D = h.shape
    tm = COMBINE_TM
    n = tm * TOP_K
    return pl.pallas_call(
        _combine_kernel,
        out_shape=jax.ShapeDtypeStruct((T, D), f32),
        grid=(T // tm,),
        in_specs=[pl.BlockSpec((n,), lambda i: (i,), memory_space=pltpu.SMEM),
                  pl.BlockSpec((tm, D), lambda i: (i, 0)),
                  pl.BlockSpec((tm, LANES), lambda i: (i, 0)),
                  pl.BlockSpec((1, D), lambda i: (0, 0)),
                  pl.BlockSpec((1, D), lambda i: (0, 0)),
                  pl.BlockSpec(memory_space=pl.ANY)],
        out_specs=pl.BlockSpec((tm, D), lambda i: (i, 0)),
        scratch_shapes=[pltpu.VMEM((TOP_K, tm, D), f32), pltpu.SemaphoreType.DMA(())],
        compiler_params=pltpu.CompilerParams(
            dimension_semantics=("arbitrary",), vmem_limit_bytes=VMEM_LIMIT),
        name="moe_combine_ln2",
    )(dest_flat, h, gate, l2w, l2b, y)


def _pack_w_in(w_in):
    o1 = 3 * ATT_WIDTH
    o2 = o1 + 3 * RWKV_WIDTH
    o3 = o2 + DECAY_LORA
    o4 = o3 + ICLR_LORA
    o5 = o4 + GATE_LORA
    padc = lambda w, n: jnp.pad(w, ((0, 0), (0, n - w.shape[1])))
    parts = [w_in[:, o5:], w_in[:, :o2], padc(w_in[:, o2:o3], LANES), padc(w_in[:, o3:o4], LANES),
             w_in[:, o4:o5]]
    return jnp.concatenate(parts, axis=1).astype(bf16)


def _token_mixing(x2, B, S, w_in, rel_bias, shift_mu, w0, w_up, a0, a_up, g_up, k_k, k_a, r_k,
                  lnx_w, lnx_b):
    T, D = x2.shape
    wp = _pack_w_in(w_in)
    proj = _in_proj(x2, wp, min(1024, T), 768)

    y_a = _attention(proj, _attn_bias_table(rel_bias), B, S)

    W = RWKV_WIDTH
    row = lambda p: p.reshape(1, -1).astype(f32)
    padv = lambda p, n: jnp.pad(p, (0, n - p.shape[0]))
    mu_rkv = row(shift_mu[:3 * W])
    mu_lora = row(jnp.concatenate([padv(shift_mu[3 * W:3 * W + DECAY_LORA], LANES),
                                   padv(shift_mu[3 * W + DECAY_LORA:3 * W + DECAY_LORA + ICLR_LORA], LANES),
                                   shift_mu[3 * W + DECAY_LORA + ICLR_LORA:]]))
    padr = lambda w: jnp.pad(w, ((0, LANES - w.shape[0]), (0, 0))).astype(bf16)
    y_b = _rwkv(proj, B, S, mu_rkv, mu_lora, row(w0), row(a0), row(k_k), row(k_a), row(r_k),
                row(lnx_w), row(lnx_b), padr(w_up), padr(a_up), g_up.astype(bf16))
    return proj, y_a, y_b


def _moe_routing(topi, rank, counts, n_experts, rb):
    T = topi.shape[0]
    M = T * TOP_K
    counts = counts.astype(i32)
    padded = (counts + rb - 1) // rb * rb
    pad_ends = jnp.cumsum(padded)
    pad_starts = pad_ends - padded
    onehot = topi[..., None] == jnp.arange(n_experts, dtype=i32)
    dest = (jnp.sum(jnp.where(onehot, pad_starts, 0), axis=-1) + rank).reshape(M)
    n_blocks = (M + n_experts * (rb - 1) + rb - 1) // rb
    blk_e = jnp.minimum(jnp.searchsorted(pad_ends, jnp.arange(n_blocks, dtype=i32) * rb, side='right'),
                        n_experts - 1).astype(i32)
    nused = (pad_ends[-1:] // rb).astype(i32)
    return dest.astype(i32), blk_e, nused, n_blocks


def kernel(x, w_in, rel_bias, shift_mu, w0, w_up, a0, a_up, g_up, k_k, k_a, r_k, lnx_w, lnx_b,
           proj_a, proj_b, w_out, ln1_w, ln1_b, w_router, b_router, w_gu, b_gu, w_down, b_down,
           ln2_w, ln2_b):
    B, S, D = x.shape
    T = B * S
    h = x.reshape(T, D)
    for l in range(DEPTH):
        proj, y_a, y_b = _token_mixing(h, B, S, w_in[l], rel_bias[l], shift_mu[l], w0[l], w_up[l],
                                       a0[l], a_up[l], g_up[l], k_k[l], k_a[l], r_k[l],
                                       lnx_w[l], lnx_b[l])
        E = w_router.shape[-1]
        h1, topi, gate, rank, counts = _merge_router(
            h, y_a, y_b, proj, proj_a[l].astype(bf16), proj_b[l].astype(bf16), w_out[l].astype(bf16),
            ln1_w[l].reshape(1, D), ln1_b[l].reshape(1, D), w_router[l], b_router[l].reshape(1, E))
        dest, blk_e, nused, n_blocks = _moe_routing(topi[:, :TOP_K], rank[:, :TOP_K], counts[0], E, MOE_RB)
        xs = _dispatch(dest, h1, jnp.zeros((n_blocks * MOE_RB, D), f32))
        y = _experts(blk_e, nused, xs, w_gu[l].astype(bf16), b_gu[l].reshape(E, 1, -1),
                     w_down[l].astype(bf16), b_down[l].reshape(E, 1, D))
        h = _combine(dest, h1, gate, ln2_w[l].reshape(1, D), ln2_b[l].reshape(1, D), y)
    return h.reshape(B, S, D)
```

```python
import jax
import jax.numpy as jnp
from jax import lax
from jax.experimental import pallas as pl
from jax.experimental.pallas import tpu as pltpu

f32 = jnp.float32
bf16 = jnp.bfloat16
i32 = jnp.int32

CHUNK = 64
LEFT_CHUNKS = 8
HEAD_DIM = 64
ATT_WIDTH = 1024
MAX_PAST_DIST = 256
REL_TABLE = MAX_PAST_DIST + CHUNK
RWKV_WIDTH = 1024
DECAY_LORA = 96
ICLR_LORA = 96
GATE_LORA = 256
TOP_K = 4
SWIGLU_LIMIT = 7.0
SWIGLU_ALPHA = 1.702
LN_EPS = 1e-5
GN_EPS = 64e-5
DEPTH = 1
DEEPNORM_ALPHA = (2 * DEPTH) ** 0.25

LANES = 128
PAIR = 2 * HEAD_DIM
BF16_SUBLANES = 16
VMEM_LIMIT = 56 * 1024 * 1024

NEG = -1e30

LORA_PAD = 512
COL_GATE = 0
COL_Q = 4096
COL_K = COL_Q + ATT_WIDTH
COL_V = COL_K + ATT_WIDTH
COL_R = COL_V + ATT_WIDTH
COL_KR = COL_R + RWKV_WIDTH
COL_VR = COL_KR + RWKV_WIDTH
COL_LORA = COL_VR + RWKV_WIDTH
PROJ_COLS = COL_LORA + LORA_PAD

ATT_TQ = 256
ATT_HB = 4
RWKV_TR = 1024
MERGE_TM = 256
MOE_RB = 512
MOE_TF = 1024
COMBINE_TM = 256
DMA_UNROLL = 4


def _dot(a, b, precision=None):
    return jnp.dot(a, b, preferred_element_type=f32, precision=precision)


def _dot_nt(a, b):
    return lax.dot_general(a, b, (((1,), (1,)), ((), ())), preferred_element_type=f32)


def _dot_tn(a, b):
    return lax.dot_general(a, b, (((0,), (0,)), ((), ())), preferred_element_type=f32)


def _split_bf16(a):
    hi = a.astype(bf16)
    return hi, (a - hi.astype(f32)).astype(bf16)


def _dot_split_lhs(a, b_bf16):
    hi, lo = _split_bf16(a)
    return _dot(hi, b_bf16) + _dot(lo, b_bf16)


def _dot_split_rhs(a_bf16, b):
    hi, lo = _split_bf16(b)
    return _dot(a_bf16, hi) + _dot(a_bf16, lo)


def _inproj_kernel(x_ref, w_ref, o_ref, xb_ref):
    @pl.when(pl.program_id(1) == 0)
    def _():
        xb_ref[...] = x_ref[...].astype(bf16)

    o_ref[...] = _dot(xb_ref[...], w_ref[...]).astype(o_ref.dtype)


def _in_proj(x2, w, tm, tn):
    M, K = x2.shape
    N = w.shape[1]
    return pl.pallas_call(
        _inproj_kernel,
        out_shape=jax.ShapeDtypeStruct((M, N), bf16),
        grid=(M // tm, N // tn),
        in_specs=[pl.BlockSpec((tm, K), lambda i, j: (i, 0)),
                  pl.BlockSpec((K, tn), lambda i, j: (0, j))],
        out_specs=pl.BlockSpec((tm, tn), lambda i, j: (i, j)),
        scratch_shapes=[pltpu.VMEM((tm, K), bf16)],
        compiler_params=pltpu.CompilerParams(
            dimension_semantics=("parallel", "arbitrary"), vmem_limit_bytes=VMEM_LIMIT),
        name="in_proj",
    )(x2, w)


def _attn_bias_table(rel_bias):
    tq = ATT_TQ
    nk = 3 * tq
    H = rel_bias.shape[0]
    span = nk + tq
    dist = jnp.arange(span - 1) - (tq - 1)
    idx = jnp.clip(jnp.minimum(dist, MAX_PAST_DIST) + (CHUNK - 1), 0, REL_TABLE - 1)
    u = jnp.pad(rel_bias.astype(f32)[:, idx], ((0, 0), (0, 1)))
    G = jnp.broadcast_to(u[:, None, :], (H, nk, span)).reshape(H, nk * span)
    G = G[:, :nk * (span - 1)].reshape(H, nk, span - 1)
    b = jnp.transpose(G[:, :, nk - 1:], (0, 2, 1))
    qi = jnp.arange(tq)[:, None]
    kj = jnp.arange(nk)[None, :]
    dchunk = (qi + 2 * tq) // CHUNK - kj // CHUNK
    ok = (dchunk >= 0) & (dchunk <= LEFT_CHUNKS)
    return jnp.where(ok[None], b, NEG)


def _attn_kernel(q_ref, k0_ref, k1_ref, k2_ref, v0_ref, v1_ref, v2_ref, bias_ref, side_ref,
                 o_ref, side_o_ref):
    qt = pl.program_id(1)
    side_o_ref[...] = side_ref[...].astype(side_o_ref.dtype)
    tq = q_ref.shape[0]
    k_refs = (k0_ref, k1_ref, k2_ref)
    v_refs = (v0_ref, v1_ref, v2_ref)
    heads = range(ATT_HB)
    blocks = range(3)
    sl = [slice(h * HEAD_DIM, (h + 1) * HEAD_DIM) for h in heads]
    q = [q_ref[:, sl[h]] * (HEAD_DIM ** -0.5) for h in heads]
    s = [[_dot_nt(q[h], k_refs[j][:, sl[h]]) + bias_ref[h, :, j * tq:(j + 1) * tq] for j in blocks]
         for h in heads]
    s = [[jnp.where(qt - 2 + j >= 0, s[h][j], NEG) if j < 2 else s[h][j] for j in blocks] for h in heads]
    m = [jnp.max(jnp.maximum(jnp.maximum(s[h][0], s[h][1]), s[h][2]), axis=-1, keepdims=True)
         for h in heads]
    p = [[jnp.exp(s[h][j] - m[h]) for j in blocks] for h in heads]
    l = [jnp.sum(p[h][0] + p[h][1] + p[h][2], axis=-1, keepdims=True) for h in heads]
    acc = [_dot(p[h][0].astype(bf16), v_refs[0][:, sl[h]])
           + _dot(p[h][1].astype(bf16), v_refs[1][:, sl[h]])
           + _dot(p[h][2].astype(bf16), v_refs[2][:, sl[h]]) for h in heads]
    o_ref[...] = jnp.concatenate([acc[h] / l[h] for h in heads], axis=1).astype(o_ref.dtype)


def _attention(proj, bias, B, S, side):
    T = B * S
    tq = ATT_TQ
    nt = S // tq
    width = ATT_HB * HEAD_DIM
    ngrp = ATT_WIDTH // width
    side_spec = pl.BlockSpec((side.shape[0] // (B * nt * ngrp), side.shape[1]),
                             lambda b, t, h: ((b * nt + t) * ngrp + h, 0))
    qcol, kcol, vcol = COL_Q // width, COL_K // width, COL_V // width

    def kv_spec(col, back):
        return pl.BlockSpec((tq, width),
                            lambda b, t, h: (b * nt + jnp.maximum(t - back, 0), col + h))

    return pl.pallas_call(
        _attn_kernel,
        out_shape=(jax.ShapeDtypeStruct((T, ATT_WIDTH), bf16), jax.ShapeDtypeStruct(side.shape, bf16)),
        grid=(B, nt, ngrp),
        in_specs=[pl.BlockSpec((tq, width), lambda b, t, h: (b * nt + t, qcol + h)),
                  kv_spec(kcol, 2), kv_spec(kcol, 1), kv_spec(kcol, 0),
                  kv_spec(vcol, 2), kv_spec(vcol, 1), kv_spec(vcol, 0),
                  pl.BlockSpec((ATT_HB, tq, 3 * tq), lambda b, t, h: (h, 0, 0)),
                  side_spec],
        out_specs=(pl.BlockSpec((tq, width), lambda b, t, h: (b * nt + t, h)), side_spec),
        compiler_params=pltpu.CompilerParams(
            dimension_semantics=("parallel", "parallel", "parallel"), vmem_limit_bytes=VMEM_LIMIT),
        name="chunk_attn",
    )(proj, proj, proj, proj, proj, proj, proj, bias, side)


def _rwkv_kernel(r_ref, k_ref, v_ref, lo_ref, rp_ref, kp_ref, vp_ref, lop_ref,
                 mur_ref, muk_ref, muv_ref, mulo_ref, w0_ref, a0_ref, kkp_ref, kap_ref, rkp_ref,
                 lnw_ref, lnb_ref, wup_ref, aup_ref, gup_ref, side_ref,
                 o_ref, side_o_ref, zero_o_ref,
                 st_ref, r_s, lw_s, k_s, v_s, a_s, b_s, g_s, bon_s):
    t = pl.program_id(2)
    tr = r_ref.shape[0]
    C = CHUNK

    side_o_ref[...] = side_ref[...].astype(side_o_ref.dtype)
    zero_o_ref[...] = jnp.zeros_like(zero_o_ref)

    @pl.when(t == 0)
    def _():
        st_ref[...] = jnp.zeros_like(st_ref)

    not_first = jnp.where(t == 0, 0.0, 1.0).astype(f32)
    row = lax.broadcasted_iota(i32, (tr, 1), 0)

    def shift(p_ref, prev_ref, mu_ref):
        p = p_ref[...].astype(f32)
        prev_row = prev_ref[BF16_SUBLANES - 1:BF16_SUBLANES, :].astype(f32) * not_first
        prev = jnp.where(row == 0, prev_row, pltpu.roll(p, 1, 0))
        return p + (prev - p) * mu_ref[...]

    r = shift(r_ref, rp_ref, mur_ref)
    k = shift(k_ref, kp_ref, muk_ref)
    v = shift(v_ref, vp_ref, muv_ref)
    lo = shift(lo_ref, lop_ref, mulo_ref)
    wd = lo[:, 0:LANES]
    ad = lo[:, LANES:2 * LANES]
    gd = lo[:, 2 * LANES:]

    ri = lax.broadcasted_iota(i32, (PAIR, PAIR), 0)
    ci = lax.broadcasted_iota(i32, (PAIR, PAIR), 1)
    same_head = (ri // HEAD_DIM) == (ci // HEAD_DIM)
    head_ones = jnp.where(same_head, 1.0, 0.0).astype(bf16)
    head_avg = jnp.where(same_head, 1.0 / HEAD_DIM, 0.0).astype(bf16)

    z = w0_ref[...] + _dot(jnp.tanh(wd).astype(bf16), wup_ref[...])
    softplus_negz = jnp.maximum(-z, 0.0) + jnp.log(1.0 + jnp.exp(-jnp.abs(z)))
    lw = -jnp.exp(-softplus_negz - 0.5)
    alr = jax.nn.sigmoid(a0_ref[...] + _dot(ad.astype(bf16), aup_ref[...]))
    g = _dot(jax.nn.sigmoid(gd).astype(bf16), gup_ref[...])
    kk = k * kkp_ref[...]
    kk = kk / jnp.maximum(jnp.sqrt(_dot_split_lhs(kk * kk, head_ones)), 1e-12)
    k2 = k * (1.0 + (alr - 1.0) * kap_ref[...])
    bonus = _dot_split_lhs(r * k2 * rkp_ref[...], head_ones) * v

    r_s[...] = r
    lw_s[...] = lw
    k_s[...] = k2
    v_s[...] = v
    a_s[...] = -kk
    b_s[...] = kk * alr
    g_s[...] = g
    bon_s[...] = bonus

    lane = lax.broadcasted_iota(i32, (1, PAIR), 1)
    head0 = lane < HEAD_DIM
    tri_incl = jnp.where(lax.broadcasted_iota(i32, (C, C), 1) <= lax.broadcasted_iota(i32, (C, C), 0),
                         1.0, 0.0).astype(bf16)
    strict = ci < ri
    incl = ci <= ri
    eye = ci == ri
    lnw = lnw_ref[...]
    lnb = lnb_ref[...]

    def stack(x):
        return jnp.concatenate([jnp.where(head0, x, 0.0), jnp.where(head0, 0.0, x)], axis=0)

    nc = tr // C
    chunks = range(nc)
    rows = [slice(c * C, (c + 1) * C) for c in chunks]
    lwc = [lw_s[rw, :] for rw in rows]
    L = [_dot_split_rhs(tri_incl, x) for x in lwc]
    LC = [x[C - 1:C, :] for x in L]
    a_t, r_t, r_tb, bk_t, v_st = [], [], [], [], []
    for c in chunks:
        rw = rows[c]
        e_neg = jnp.exp(-L[c])
        a_t.append(stack(a_s[rw, :] * jnp.exp(L[c] - lwc[c])).astype(bf16))
        r_t.append(stack(r_s[rw, :] * jnp.exp(L[c])))
        r_tb.append(r_t[c].astype(bf16))
        bk_t.append(jnp.concatenate([stack(b_s[rw, :] * e_neg), stack(k_s[rw, :] * e_neg)], axis=0).astype(bf16))
        v_st.append(stack(v_s[rw, :]).astype(bf16))

    A = [_dot_nt(jnp.concatenate([a_t[c], r_tb[c]], axis=0), bk_t[c]) for c in chunks]
    a_ab = [jnp.where(strict, x[0:PAIR, 0:PAIR], 0.0) for x in A]
    a_kk = [jnp.concatenate([jnp.where(strict, x[0:PAIR, PAIR:], 0.0),
                             jnp.where(incl, x[PAIR:, PAIR:], 0.0)], axis=0).astype(bf16) for x in A]
    a_rb = [jnp.where(incl, x[PAIR:, 0:PAIR], 0.0).astype(bf16) for x in A]

    p = a_ab
    tinv = [jnp.where(eye, 1.0, 0.0).astype(f32) + x for x in a_ab]
    for _ in range(5):
        pb = [x.astype(bf16) for x in p]
        p = [_dot(x, x) for x in pb]
        tinv = [tv + _dot(tv.astype(bf16), x.astype(bf16)) for tv, x in zip(tinv, p)]

    av = [_dot(a_kk[c], v_st[c]) for c in chunks]
    wu = [_dot(tinv[c].astype(bf16), jnp.concatenate([a_t[c], av[c][0:PAIR].astype(bf16)], axis=1))
          for c in chunks]
    wub = [x.astype(bf16) for x in wu]
    ry = [jnp.concatenate([r_t[c], av[c][PAIR:]], axis=1) + _dot(a_rb[c], wub[c]) for c in chunks]
    ry = [x[0:C] + x[C:] for x in ry]

    mct, gct = [], []
    for c in chunks:
        rw = rows[c]
        e_end = jnp.exp(LC[c] - L[c])
        b_h = stack(b_s[rw, :] * e_end).astype(bf16)
        k_h = stack(k_s[rw, :] * e_end).astype(bf16)
        mct.append(jnp.where(eye, jnp.exp(LC[c]), 0.0) + _dot_tn(b_h, wub[c][:, 0:PAIR]))
        gct.append(_dot_tn(jnp.concatenate([b_h, k_h], axis=0),
                           jnp.concatenate([wub[c][:, PAIR:], v_st[c]], axis=0)))

    st = st_ref[...]
    y = []
    for c in chunks:
        stb = st.astype(bf16)
        y.append(_dot(ry[c][:, 0:PAIR].astype(bf16), stb) + ry[c][:, PAIR:])
        st = _dot(mct[c].astype(bf16), stb) + gct[c]
    st_ref[...] = st

    mu = [_dot_split_lhs(x, head_avg) for x in y]
    d = [x - m for x, m in zip(y, mu)]
    var = [_dot_split_lhs(x * x, head_avg) for x in d]
    for c in chunks:
        yn = d[c] * lax.rsqrt(var[c] + GN_EPS) * lnw + lnb
        o_ref[rows[c], :] = ((yn + bon_s[rows[c], :]) * g_s[rows[c], :]).astype(o_ref.dtype)


def _rwkv(proj, B, S, mu_rkv, mu_lora, w0, a0, k_k, k_a, r_k, lnx_w, lnx_b, w_up_p, a_up_p, g_up, side,
          zero_shape):
    T = B * S
    tr = RWKV_TR
    nt = S // tr
    sub = tr // BF16_SUBLANES
    nsub = S // BF16_SUBLANES
    npair = RWKV_WIDTH // PAIR
    rcol, kcol, vcol = COL_R // PAIR, COL_KR // PAIR, COL_VR // PAIR
    locol = COL_LORA // LORA_PAD
    side_rows = side.shape[0] // (B * npair * nt)
    zero_rows = zero_shape[0] // (B * npair * nt)
    step = lambda b, h, t: ((b * npair + h) * nt + t, 0)

    def cur(col):
        return pl.BlockSpec((tr, PAIR), lambda b, h, t: (b * nt + t, col + h))

    def prev(col):
        return pl.BlockSpec((BF16_SUBLANES, PAIR),
                            lambda b, h, t: (b * nsub + jnp.maximum(t * sub - 1, 0), col + h))

    def vec(off):
        return pl.BlockSpec((1, PAIR), lambda b, h, t: (0, off + h))

    in_specs = [
        cur(rcol), cur(kcol), cur(vcol),
        pl.BlockSpec((tr, LORA_PAD), lambda b, h, t: (b * nt + t, locol)),
        prev(rcol), prev(kcol), prev(vcol),
        pl.BlockSpec((BF16_SUBLANES, LORA_PAD),
                     lambda b, h, t: (b * nsub + jnp.maximum(t * sub - 1, 0), locol)),
        vec(0), vec(npair), vec(2 * npair),
        pl.BlockSpec((1, LORA_PAD), lambda b, h, t: (0, 0)),
        vec(0), vec(0), vec(0), vec(0), vec(0), vec(0), vec(0),
        pl.BlockSpec((LANES, PAIR), lambda b, h, t: (0, h)),
        pl.BlockSpec((LANES, PAIR), lambda b, h, t: (0, h)),
        pl.BlockSpec((GATE_LORA, PAIR), lambda b, h, t: (0, h)),
        pl.BlockSpec((side_rows, side.shape[1]), step),
    ]
    scratch = [pltpu.VMEM((PAIR, PAIR), f32)] + [pltpu.VMEM((tr, PAIR), f32)] * 8
    return pl.pallas_call(
        _rwkv_kernel,
        out_shape=(jax.ShapeDtypeStruct((T, RWKV_WIDTH), bf16),
                   jax.ShapeDtypeStruct(side.shape, bf16),
                   jax.ShapeDtypeStruct(zero_shape, f32)),
        grid=(B, npair, nt),
        in_specs=in_specs,
        out_specs=(pl.BlockSpec((tr, PAIR), lambda b, h, t: (b * nt + t, h)),
                   pl.BlockSpec((side_rows, side.shape[1]), step),
                   pl.BlockSpec((zero_rows, zero_shape[1]), step)),
        scratch_shapes=scratch,
        compiler_params=pltpu.CompilerParams(
            dimension_semantics=("parallel", "parallel", "arbitrary"), vmem_limit_bytes=VMEM_LIMIT),
        name="rwkv7",
    )(proj, proj, proj, proj, proj, proj, proj, proj,
      mu_rkv, mu_rkv, mu_rkv, mu_lora, w0, a0, k_k, k_a, r_k, lnx_w, lnx_b, w_up_p, a_up_p, g_up, side)


def _layer_norm(z, w, b):
    mu = jnp.mean(z, axis=-1, keepdims=True)
    d = z - mu
    var = jnp.mean(d * d, axis=-1, keepdims=True)
    return d * lax.rsqrt(var + LN_EPS) * w + b


def _merge_kernel(x_ref, ya_ref, yb_ref, ga_ref, gb_ref, pa_ref, pb_ref, wo_ref, l1w_ref, l1b_ref,
                  wr_ref, br_ref,
                  h_ref, topi_ref, gate_ref, rank_ref, cnt_ref,
                  run_ref):
    i = pl.program_id(0)
    tm = x_ref.shape[0]
    ne = wr_ref.shape[1]

    @pl.when(i == 0)
    def _():
        run_ref[...] = jnp.zeros_like(run_ref)

    ma = _dot(ya_ref[...], pa_ref[...])
    mb = _dot(yb_ref[...], pb_ref[...])
    merged = (jax.nn.sigmoid(ga_ref[...].astype(f32)) * ma
              + jax.nn.sigmoid(gb_ref[...].astype(f32)) * mb)
    mix = _dot(merged.astype(bf16), wo_ref[...])
    h = _layer_norm(DEEPNORM_ALPHA * x_ref[...] + mix, l1w_ref[...], l1b_ref[...])
    h_ref[...] = h

    h_hi, h_lo = _split_bf16(h)
    w_hi, w_lo = _split_bf16(wr_ref[...])
    logits = _dot(h_hi, w_hi) + (_dot(h_lo, w_hi) + _dot(h_hi, w_lo)) + br_ref[...]
    eidx = lax.broadcasted_iota(i32, (tm, ne), 1)
    slot = lax.broadcasted_iota(i32, (tm, LANES), 1)
    lg = logits
    vals, idxs = [], []
    for _ in range(TOP_K):
        m = jnp.max(lg, axis=-1, keepdims=True)
        idx = jnp.min(jnp.where(lg == m, eidx, ne), axis=-1, keepdims=True)
        vals.append(m)
        idxs.append(idx)
        lg = jnp.where(eidx == idx, -jnp.inf, lg)
    exps = [jnp.exp(vk - vals[0]) for vk in vals]
    denom = exps[0] + exps[1] + exps[2] + exps[3]

    onehots = [jnp.where(eidx == idx, 1.0, 0.0).astype(f32) for idx in idxs]
    oh = onehots[0] + onehots[1] + onehots[2] + onehots[3]
    tri = jnp.where(lax.broadcasted_iota(i32, (tm, tm), 1) < lax.broadcasted_iota(i32, (tm, tm), 0),
                    1.0, 0.0).astype(bf16)
    before = run_ref[...] + _dot(tri, oh.astype(bf16))

    topi_o = jnp.zeros((tm, LANES), i32)
    gate_o = jnp.zeros((tm, LANES), f32)
    rank_o = jnp.zeros((tm, LANES), i32)
    for kslot in range(TOP_K):
        rk = jnp.sum(onehots[kslot] * before, axis=-1, keepdims=True).astype(i32)
        topi_o = jnp.where(slot == kslot, idxs[kslot], topi_o)
        gate_o = jnp.where(slot == kslot, exps[kslot] / denom, gate_o)
        rank_o = jnp.where(slot == kslot, rk, rank_o)
    topi_ref[...] = topi_o
    gate_ref[...] = gate_o
    rank_ref[...] = rank_o
    run = run_ref[...] + jnp.sum(oh, axis=0, keepdims=True)
    run_ref[...] = run
    cnt_ref[...] = run


def _merge_router(x2, ya, yb, proj, pa, pb, wo, l1w, l1b, wr, br):
    T, D = x2.shape
    tm = MERGE_TM
    ne = wr.shape[1]
    gcol = COL_GATE // D

    def const(shape):
        return pl.BlockSpec(shape, lambda i: (0, 0), pipeline_mode=pl.Buffered(1))

    out_shape = (jax.ShapeDtypeStruct((T, D), f32),
                 jax.ShapeDtypeStruct((T, LANES), i32),
                 jax.ShapeDtypeStruct((T, LANES), f32),
                 jax.ShapeDtypeStruct((T, LANES), i32),
                 jax.ShapeDtypeStruct((1, ne), f32))
    return pl.pallas_call(
        _merge_kernel,
        out_shape=out_shape,
        grid=(T // tm,),
        in_specs=[pl.BlockSpec((tm, D), lambda i: (i, 0)),
                  pl.BlockSpec((tm, ATT_WIDTH), lambda i: (i, 0)),
                  pl.BlockSpec((tm, RWKV_WIDTH), lambda i: (i, 0)),
                  pl.BlockSpec((tm, D), lambda i: (i, gcol)),
                  pl.BlockSpec((tm, D), lambda i: (i, gcol + 1)),
                  const(pa.shape), const(pb.shape), const(wo.shape),
                  const((1, D)), const((1, D)), const(wr.shape), const((1, ne))],
        out_specs=(pl.BlockSpec((tm, D), lambda i: (i, 0)),
                   pl.BlockSpec((tm, LANES), lambda i: (i, 0)),
                   pl.BlockSpec((tm, LANES), lambda i: (i, 0)),
                   pl.BlockSpec((tm, LANES), lambda i: (i, 0)),
                   pl.BlockSpec((1, ne), lambda i: (0, 0))),
        scratch_shapes=[pltpu.VMEM((1, ne), f32)],
        compiler_params=pltpu.CompilerParams(
            dimension_semantics=("arbitrary",), vmem_limit_bytes=VMEM_LIMIT),
        name="merge_ln1_router",
    )(x2, ya, yb, proj, proj, pa, pb, wo, l1w, l1b, wr, br)


def _dispatch_kernel(dest_ref, h_ref, xs_in_hbm, xs_hbm, sem):
    del xs_in_hbm
    tm = h_ref.shape[0]

    def row_copy(t, k):
        return pltpu.make_async_copy(h_ref.at[pl.ds(t, 1)],
                                     xs_hbm.at[pl.ds(dest_ref[t * TOP_K + k], 1)], sem)

    def start(t, c):
        for k in range(TOP_K):
            row_copy(t, k).start()
        return c

    def wait(t, c):
        for k in range(TOP_K):
            row_copy(t, k).wait()
        return c

    lax.fori_loop(0, tm, start, 0, unroll=DMA_UNROLL)
    lax.fori_loop(0, tm, wait, 0, unroll=DMA_UNROLL)


def _dispatch(dest_flat, h, xs0):
    T, D = h.shape
    n = COMBINE_TM * TOP_K
    return pl.pallas_call(
        _dispatch_kernel,
        out_shape=jax.ShapeDtypeStruct(xs0.shape, xs0.dtype),
        grid=(T * TOP_K // n,),
        in_specs=[pl.BlockSpec((n,), lambda i: (i,), memory_space=pltpu.SMEM),
                  pl.BlockSpec((n // TOP_K, D), lambda i: (i, 0)),
                  pl.BlockSpec(memory_space=pl.ANY)],
        out_specs=pl.BlockSpec(memory_space=pl.ANY),
        scratch_shapes=[pltpu.SemaphoreType.DMA(())],
        input_output_aliases={2: 0},
        compiler_params=pltpu.CompilerParams(
            dimension_semantics=("arbitrary",), has_side_effects=True, vmem_limit_bytes=VMEM_LIMIT),
        name="moe_dispatch",
    )(dest_flat, h, xs0)


def _expert_kernel(be_ref, nu_ref, x_ref, wg_ref, wu_ref, bg_ref, bu_ref, wd_ref, bd_ref, o_ref, xb_ref):
    del be_ref
    i = pl.program_id(0)
    j = pl.program_id(1)
    used = i < nu_ref[0]

    @pl.when(jnp.logical_and(used, j == 0))
    def _():
        xb_ref[...] = x_ref[...].astype(bf16)
        o_ref[...] = jnp.broadcast_to(bd_ref[0], o_ref.shape)

    @pl.when(jnp.logical_and(jnp.logical_not(used), j == 0))
    def _():
        o_ref[...] = jnp.zeros_like(o_ref)

    @pl.when(used)
    def _():
        xb = xb_ref[...]
        g = _dot(xb, wg_ref[0]) + bg_ref[0]
        u = _dot(xb, wu_ref[0]) + bu_ref[0]
        g = jnp.minimum(g, SWIGLU_LIMIT)
        u = jnp.clip(u, -SWIGLU_LIMIT, SWIGLU_LIMIT)
        act = (u + 1.0) * (g * jax.nn.sigmoid(SWIGLU_ALPHA * g))
        o_ref[...] += _dot(act.astype(bf16), wd_ref[0])


def _experts(blk_e, nused, xs, w_gu, b_gu, w_down, b_down):
    R, D = xs.shape
    E, _, F2 = w_gu.shape
    F = F2 // 2
    rb, tf = MOE_RB, min(MOE_TF, F)
    nf = F // tf

    def jj(i, j, nu):
        return jnp.where(i < nu[0], j, 0)

    grid_spec = pltpu.PrefetchScalarGridSpec(
        num_scalar_prefetch=2,
        grid=(R // rb, nf),
        in_specs=[pl.BlockSpec((rb, D), lambda i, j, be, nu: (jnp.where(i < nu[0], i, 0), 0)),
                  pl.BlockSpec((1, D, tf), lambda i, j, be, nu: (be[i], 0, jj(i, j, nu))),
                  pl.BlockSpec((1, D, tf), lambda i, j, be, nu: (be[i], 0, nf + jj(i, j, nu))),
                  pl.BlockSpec((1, 1, tf), lambda i, j, be, nu: (be[i], 0, jj(i, j, nu))),
                  pl.BlockSpec((1, 1, tf), lambda i, j, be, nu: (be[i], 0, nf + jj(i, j, nu))),
                  pl.BlockSpec((1, tf, D), lambda i, j, be, nu: (be[i], jj(i, j, nu), 0)),
                  pl.BlockSpec((1, 1, D), lambda i, j, be, nu: (be[i], 0, 0))],
        out_specs=pl.BlockSpec((rb, D), lambda i, j, be, nu: (i, 0)),
        scratch_shapes=[pltpu.VMEM((rb, D), bf16)],
    )
    return pl.pallas_call(
        _expert_kernel,
        out_shape=jax.ShapeDtypeStruct((R, D), f32),
        grid_spec=grid_spec,
        compiler_params=pltpu.CompilerParams(
            dimension_semantics=("parallel", "arbitrary"), vmem_limit_bytes=VMEM_LIMIT),
        name="moe_experts",
    )(blk_e, nused, xs, w_gu, w_gu, b_gu, b_gu, w_down, b_down)


def _combine_kernel(dest_ref, h_ref, gate_ref, l2w_ref, l2b_ref, y_hbm, o_ref, buf, sem):
    tm = h_ref.shape[0]

    def row_copy(t, k):
        return pltpu.make_async_copy(y_hbm.at[pl.ds(dest_ref[t * TOP_K + k], 1)],
                                     buf.at[k, pl.ds(t, 1)], sem)

    def start(t, c):
        for k in range(TOP_K):
            row_copy(t, k).start()
        return c

    def wait(t, c):
        for k in range(TOP_K):
            row_copy(t, k).wait()
        return c

    lax.fori_loop(0, tm, start, 0, unroll=DMA_UNROLL)
    lax.fori_loop(0, tm, wait, 0, unroll=DMA_UNROLL)

    gate = gate_ref[...]
    ffn = gate[:, 0:1] * buf[0]
    for kslot in range(1, TOP_K):
        ffn = ffn + gate[:, kslot:kslot + 1] * buf[kslot]
    o_ref[...] = _layer_norm(DEEPNORM_ALPHA * h_ref[...] + ffn, l2w_ref[...], l2b_ref[...])


def _combine(dest_flat, h, gate, l2w, l2b, y):
    T, D = h.shape
    tm = COMBINE_TM
    n = tm * TOP_K
    return pl.pallas_call(
        _combine_kernel,
        out_shape=jax.ShapeDtypeStruct((T, D), f32),
        grid=(T // tm,),
        in_specs=[pl.BlockSpec((n,), lambda i: (i,), memory_space=pltpu.SMEM),
                  pl.BlockSpec((tm, D), lambda i: (i, 0)),
                  pl.BlockSpec((tm, LANES), lambda i: (i, 0)),
                  pl.BlockSpec((1, D), lambda i: (0, 0)),
                  pl.BlockSpec((1, D), lambda i: (0, 0)),
                  pl.BlockSpec(memory_space=pl.ANY)],
        out_specs=pl.BlockSpec((tm, D), lambda i: (i, 0)),
        scratch_shapes=[pltpu.VMEM((TOP_K, tm, D), f32), pltpu.SemaphoreType.DMA(())],
        compiler_params=pltpu.CompilerParams(
            dimension_semantics=("arbitrary",), vmem_limit_bytes=VMEM_LIMIT),
        name="moe_combine_ln2",
    )(dest_flat, h, gate, l2w, l2b, y)


def _pack_w_in(w_in):
    o1 = 3 * ATT_WIDTH
    o2 = o1 + 3 * RWKV_WIDTH
    o3 = o2 + DECAY_LORA
    o4 = o3 + ICLR_LORA
    o5 = o4 + GATE_LORA
    padc = lambda w, n: jnp.pad(w, ((0, 0), (0, n - w.shape[1])))
    parts = [w_in[:, o5:], w_in[:, :o2], padc(w_in[:, o2:o3], LANES), padc(w_in[:, o3:o4], LANES),
             w_in[:, o4:o5]]
    return jnp.concatenate(parts, axis=1).astype(bf16)


def _token_mixing(x2, B, S, w_in, rel_bias, shift_mu, w0, w_up, a0, a_up, g_up, k_k, k_a, r_k,
                  lnx_w, lnx_b, side_a, side_b, zero_shape):
    T, D = x2.shape
    wp = _pack_w_in(w_in)
    proj = _in_proj(x2, wp, min(1024, T), 768)

    y_a, side_a = _attention(proj, _attn_bias_table(rel_bias), B, S, side_a)

    W = RWKV_WIDTH
    row = lambda p: p.reshape(1, -1).astype(f32)
    padv = lambda p, n: jnp.pad(p, (0, n - p.shape[0]))
    mu_rkv = row(shift_mu[:3 * W])
    mu_lora = row(jnp.concatenate([padv(shift_mu[3 * W:3 * W + DECAY_LORA], LANES),
                                   padv(shift_mu[3 * W + DECAY_LORA:3 * W + DECAY_LORA + ICLR_LORA], LANES),
                                   shift_mu[3 * W + DECAY_LORA + ICLR_LORA:]]))
    padr = lambda w: jnp.pad(w, ((0, LANES - w.shape[0]), (0, 0))).astype(bf16)
    y_b, side_b, zeros = _rwkv(proj, B, S, mu_rkv, mu_lora, row(w0), row(a0), row(k_k), row(k_a), row(r_k),
                               row(lnx_w), row(lnx_b), padr(w_up), padr(a_up), g_up.astype(bf16), side_b,
                               zero_shape)
    return proj, y_a, y_b, side_a, side_b, zeros


def _moe_routing(topi, rank, counts, n_experts, rb, n_blocks):
    M = topi.shape[0] * TOP_K
    counts = counts.astype(i32)
    padded = (counts + rb - 1) // rb * rb
    pad_ends = jnp.cumsum(padded)
    pad_starts = pad_ends - padded
    onehot = topi[..., None] == jnp.arange(n_experts, dtype=i32)
    dest = (jnp.sum(jnp.where(onehot, pad_starts, 0), axis=-1) + rank).reshape(M)
    blk_start = jnp.arange(n_blocks, dtype=i32)[:, None] * rb
    blk_e = jnp.minimum(jnp.sum((pad_ends[None, :] <= blk_start).astype(i32), axis=-1), n_experts - 1)
    nused = (pad_ends[-1:] // rb).astype(i32)
    return dest.astype(i32), blk_e, nused


def kernel(x, w_in, rel_bias, shift_mu, w0, w_up, a0, a_up, g_up, k_k, k_a, r_k, lnx_w, lnx_b,
           proj_a, proj_b, w_out, ln1_w, ln1_b, w_router, b_router, w_gu, b_gu, w_down, b_down,
           ln2_w, ln2_b):
    B, S, D = x.shape
    T = B * S
    h = x.reshape(T, D)
    for l in range(DEPTH):
        E, F = w_down.shape[1], w_down.shape[2]
        n_blocks = (T * TOP_K + E * (MOE_RB - 1) + MOE_RB - 1) // MOE_RB
        proj, y_a, y_b, w_down_b, w_gu_b, xs0 = _token_mixing(
            h, B, S, w_in[l], rel_bias[l], shift_mu[l], w0[l], w_up[l], a0[l], a_up[l], g_up[l],
            k_k[l], k_a[l], r_k[l], lnx_w[l], lnx_b[l],
            w_down[l].reshape(E * F, D), w_gu[l].reshape(E * D, 2 * F), (n_blocks * MOE_RB, D))
        h1, topi, gate, rank, counts = _merge_router(
            h, y_a, y_b, proj, proj_a[l].astype(bf16), proj_b[l].astype(bf16), w_out[l].astype(bf16),
            ln1_w[l].reshape(1, D), ln1_b[l].reshape(1, D), w_router[l], b_router[l].reshape(1, E))
        dest, blk_e, nused = _moe_routing(topi[:, :TOP_K], rank[:, :TOP_K], counts[0], E, MOE_RB, n_blocks)
        xs = _dispatch(dest, h1, xs0)
        y = _experts(blk_e, nused, xs, w_gu_b.reshape(E, D, 2 * F), b_gu[l].reshape(E, 1, -1),
                     w_down_b.reshape(E, F, D), b_down[l].reshape(E, 1, D))
        h = _combine(dest, h1, gate, ln2_w[l].reshape(1, D), ln2_b[l].reshape(1, D), y)
    return h.reshape(B, S, D)
```

```python
import jax
import jax.numpy as jnp
from jax import lax
from jax.experimental import pallas as pl
from jax.experimental.pallas import tpu as pltpu

f32 = jnp.float32
bf16 = jnp.bfloat16
i32 = jnp.int32

CHUNK = 64
LEFT_CHUNKS = 8
HEAD_DIM = 64
ATT_WIDTH = 1024
MAX_PAST_DIST = 256
REL_TABLE = MAX_PAST_DIST + CHUNK
RWKV_WIDTH = 1024
DECAY_LORA = 96
ICLR_LORA = 96
GATE_LORA = 256
TOP_K = 4
SWIGLU_LIMIT = 7.0
SWIGLU_ALPHA = 1.702
LN_EPS = 1e-5
GN_EPS = 64e-5
DEPTH = 1
DEEPNORM_ALPHA = (2 * DEPTH) ** 0.25

LANES = 128
PAIR = 2 * HEAD_DIM
BF16_SUBLANES = 16
VMEM_LIMIT = 56 * 1024 * 1024

NEG = -1e30

LORA_PAD = 512
COL_GATE = 0
COL_Q = 4096
COL_K = COL_Q + ATT_WIDTH
COL_V = COL_K + ATT_WIDTH
COL_R = COL_V + ATT_WIDTH
COL_KR = COL_R + RWKV_WIDTH
COL_VR = COL_KR + RWKV_WIDTH
COL_LORA = COL_VR + RWKV_WIDTH
PROJ_COLS = COL_LORA + LORA_PAD

ATT_TQ = 256
ATT_HB = 4
RWKV_TR = 1024
MERGE_TM = 256
MOE_RB = 512
MOE_TF = 1024
COMBINE_TM = 256
DMA_UNROLL = 4


def _dot(a, b, precision=None):
    return jnp.dot(a, b, preferred_element_type=f32, precision=precision)


def _dot_nt(a, b):
    return lax.dot_general(a, b, (((1,), (1,)), ((), ())), preferred_element_type=f32)


def _dot_tn(a, b):
    return lax.dot_general(a, b, (((0,), (0,)), ((), ())), preferred_element_type=f32)


def _split_bf16(a):
    hi = a.astype(bf16)
    return hi, (a - hi.astype(f32)).astype(bf16)


def _dot_split_lhs(a, b_bf16):
    hi, lo = _split_bf16(a)
    return _dot(hi, b_bf16) + _dot(lo, b_bf16)


def _dot_split_rhs(a_bf16, b):
    hi, lo = _split_bf16(b)
    return _dot(a_bf16, hi) + _dot(a_bf16, lo)


def _inproj_kernel(x_ref, w_ref, o_ref, xb_ref):
    @pl.when(pl.program_id(1) == 0)
    def _():
        xb_ref[...] = x_ref[...].astype(bf16)

    o_ref[...] = _dot(xb_ref[...], w_ref[...]).astype(o_ref.dtype)


def _in_proj(x2, w, tm, tn):
    M, K = x2.shape
    N = w.shape[1]
    return pl.pallas_call(
        _inproj_kernel,
        out_shape=jax.ShapeDtypeStruct((M, N), bf16),
        grid=(M // tm, N // tn),
        in_specs=[pl.BlockSpec((tm, K), lambda i, j: (i, 0)),
                  pl.BlockSpec((K, tn), lambda i, j: (0, j))],
        out_specs=pl.BlockSpec((tm, tn), lambda i, j: (i, j)),
        scratch_shapes=[pltpu.VMEM((tm, K), bf16)],
        compiler_params=pltpu.CompilerParams(
            dimension_semantics=("parallel", "arbitrary"), vmem_limit_bytes=VMEM_LIMIT),
        name="in_proj",
    )(x2, w)


def _attn_bias_table(rel_bias):
    tq = ATT_TQ
    nk = 3 * tq
    H = rel_bias.shape[0]
    span = nk + tq
    dist = jnp.arange(span - 1) - (tq - 1)
    idx = jnp.clip(jnp.minimum(dist, MAX_PAST_DIST) + (CHUNK - 1), 0, REL_TABLE - 1)
    u = jnp.pad(rel_bias.astype(f32)[:, idx], ((0, 0), (0, 1)))
    G = jnp.broadcast_to(u[:, None, :], (H, nk, span)).reshape(H, nk * span)
    G = G[:, :nk * (span - 1)].reshape(H, nk, span - 1)
    b = jnp.transpose(G[:, :, nk - 1:], (0, 2, 1))
    qi = jnp.arange(tq)[:, None]
    kj = jnp.arange(nk)[None, :]
    dchunk = (qi + 2 * tq) // CHUNK - kj // CHUNK
    ok = (dchunk >= 0) & (dchunk <= LEFT_CHUNKS)
    return jnp.where(ok[None], b, NEG)


def _attn_kernel(q_ref, k0_ref, k1_ref, k2_ref, v0_ref, v1_ref, v2_ref, bias_ref, side_ref,
                 o_ref, side_o_ref):
    qt = pl.program_id(2)
    side_o_ref[...] = side_ref[...].astype(side_o_ref.dtype)
    tq = q_ref.shape[0]
    k_refs = (k0_ref, k1_ref, k2_ref)
    v_refs = (v0_ref, v1_ref, v2_ref)
    heads = range(ATT_HB)
    blocks = range(3)
    lane = lax.broadcasted_iota(i32, (1, PAIR), 1)
    head0 = lane < HEAD_DIM
    pair = [slice((h // 2) * PAIR, (h // 2 + 1) * PAIR) for h in heads]
    q = [jnp.where(head0 if h % 2 == 0 else jnp.logical_not(head0),
                   q_ref[:, pair[h]] * (HEAD_DIM ** -0.5), 0.0).astype(bf16) for h in heads]
    s = [[_dot_nt(q[h], k_refs[j][:, pair[h]]) + bias_ref[h, :, j * tq:(j + 1) * tq] for j in blocks]
         for h in heads]
    s = [[jnp.where(qt - 2 + j >= 0, s[h][j], NEG) if j < 2 else s[h][j] for j in blocks] for h in heads]
    m = [jnp.max(jnp.maximum(jnp.maximum(s[h][0], s[h][1]), s[h][2]), axis=-1, keepdims=True)
         for h in heads]
    p = [[jnp.exp((s[h][j] - m[h]).astype(bf16)) for j in blocks] for h in heads]
    ones = jnp.ones((tq, PAIR), bf16)
    v1 = [[jnp.concatenate([v_refs[j][:, pair[2 * g]], ones], axis=1) for j in blocks]
          for g in range(ATT_HB // 2)]
    acc = [_dot(p[h][0], v1[h // 2][0]) + _dot(p[h][1], v1[h // 2][1]) + _dot(p[h][2], v1[h // 2][2])
           for h in heads]
    out = [acc[h][:, 0:PAIR] / acc[h][:, PAIR:PAIR + 1] for h in heads]
    for g in range(ATT_HB // 2):
        o_ref[:, pair[2 * g]] = jnp.where(head0, out[2 * g], out[2 * g + 1]).astype(o_ref.dtype)


def _attention(proj, bias, B, S, side):
    T = B * S
    tq = ATT_TQ
    nt = S // tq
    width = ATT_HB * HEAD_DIM
    ngrp = ATT_WIDTH // width
    side_spec = pl.BlockSpec((side.shape[0] // (B * nt * ngrp), side.shape[1]),
                             lambda h, b, t: ((h * B + b) * nt + t, 0))
    qcol, kcol, vcol = COL_Q // width, COL_K // width, COL_V // width

    def kv_spec(col, back):
        return pl.BlockSpec((tq, width),
                            lambda h, b, t: (b * nt + jnp.maximum(t - back, 0), col + h))

    return pl.pallas_call(
        _attn_kernel,
        out_shape=(jax.ShapeDtypeStruct((T, ATT_WIDTH), bf16), jax.ShapeDtypeStruct(side.shape, bf16)),
        grid=(ngrp, B, nt),
        in_specs=[pl.BlockSpec((tq, width), lambda h, b, t: (b * nt + t, qcol + h)),
                  kv_spec(kcol, 2), kv_spec(kcol, 1), kv_spec(kcol, 0),
                  kv_spec(vcol, 2), kv_spec(vcol, 1), kv_spec(vcol, 0),
                  pl.BlockSpec((ATT_HB, tq, 3 * tq), lambda h, b, t: (h, 0, 0)),
                  side_spec],
        out_specs=(pl.BlockSpec((tq, width), lambda h, b, t: (b * nt + t, h)), side_spec),
        compiler_params=pltpu.CompilerParams(
            dimension_semantics=("parallel", "parallel", "parallel"), vmem_limit_bytes=VMEM_LIMIT),
        name="chunk_attn",
    )(proj, proj, proj, proj, proj, proj, proj, bias, side)


def _rwkv_kernel(r_ref, k_ref, v_ref, lo_ref, rp_ref, kp_ref, vp_ref, lop_ref,
                 mur_ref, muk_ref, muv_ref, mulo_ref, w0_ref, a0_ref, kkp_ref, kap_ref, rkp_ref,
                 lnw_ref, lnb_ref, wup_ref, aup_ref, gup_ref, side_ref,
                 o_ref, side_o_ref, zero_o_ref,
                 st_ref, r_s, lw_s, k_s, v_s, a_s, b_s, g_s, bon_s):
    t = pl.program_id(2)
    tr = r_ref.shape[0]
    C = CHUNK

    side_o_ref[...] = side_ref[...].astype(side_o_ref.dtype)
    zero_o_ref[...] = jnp.zeros_like(zero_o_ref)

    @pl.when(t == 0)
    def _():
        st_ref[...] = jnp.zeros_like(st_ref)

    not_first = jnp.where(t == 0, 0.0, 1.0).astype(f32)
    row = lax.broadcasted_iota(i32, (tr, 1), 0)

    def shift(p_ref, prev_ref, mu_ref):
        p = p_ref[...].astype(f32)
        prev_row = prev_ref[BF16_SUBLANES - 1:BF16_SUBLANES, :].astype(f32) * not_first
        prev = jnp.where(row == 0, prev_row, pltpu.roll(p, 1, 0))
        return p + (prev - p) * mu_ref[...]

    r = shift(r_ref, rp_ref, mur_ref)
    k = shift(k_ref, kp_ref, muk_ref)
    v = shift(v_ref, vp_ref, muv_ref)
    lo = shift(lo_ref, lop_ref, mulo_ref)
    wd = lo[:, 0:LANES]
    ad = lo[:, LANES:2 * LANES]
    gd = lo[:, 2 * LANES:]

    ri = lax.broadcasted_iota(i32, (PAIR, PAIR), 0)
    ci = lax.broadcasted_iota(i32, (PAIR, PAIR), 1)
    same_head = (ri // HEAD_DIM) == (ci // HEAD_DIM)
    head_ones = jnp.where(same_head, 1.0, 0.0).astype(bf16)
    head_avg = jnp.where(same_head, 1.0 / HEAD_DIM, 0.0).astype(bf16)

    z = w0_ref[...] + _dot(jnp.tanh(wd).astype(bf16), wup_ref[...])
    softplus_negz = jnp.maximum(-z, 0.0) + jnp.log(1.0 + jnp.exp(-jnp.abs(z)))
    lw = -jnp.exp(-softplus_negz - 0.5)
    alr = jax.nn.sigmoid(a0_ref[...] + _dot(ad.astype(bf16), aup_ref[...]))
    g = _dot(jax.nn.sigmoid(gd).astype(bf16), gup_ref[...])
    kk = k * kkp_ref[...]
    kk = kk / jnp.maximum(jnp.sqrt(_dot_split_lhs(kk * kk, head_ones)), 1e-12)
    k2 = k * (1.0 + (alr - 1.0) * kap_ref[...])
    bonus = _dot_split_lhs(r * k2 * rkp_ref[...], head_ones) * v

    r_s[...] = r
    lw_s[...] = lw
    k_s[...] = k2
    v_s[...] = v
    a_s[...] = -kk
    b_s[...] = kk * alr
    g_s[...] = g
    bon_s[...] = bonus

    lane = lax.broadcasted_iota(i32, (1, PAIR), 1)
    head0 = lane < HEAD_DIM
    tri_incl = jnp.where(lax.broadcasted_iota(i32, (C, C), 1) <= lax.broadcasted_iota(i32, (C, C), 0),
                         1.0, 0.0).astype(bf16)
    strict = ci < ri
    incl = ci <= ri
    eye = ci == ri
    lnw = lnw_ref[...]
    lnb = lnb_ref[...]

    def stack(x):
        return jnp.concatenate([jnp.where(head0, x, 0.0), jnp.where(head0, 0.0, x)], axis=0)

    nc = tr // C
    chunks = range(nc)
    rows = [slice(c * C, (c + 1) * C) for c in chunks]
    lwc = [lw_s[rw, :] for rw in rows]
    L = [_dot_split_rhs(tri_incl, x) for x in lwc]
    LC = [x[C - 1:C, :] for x in L]
    a_t, r_t, r_tb, bk_t, v_st = [], [], [], [], []
    for c in chunks:
        rw = rows[c]
        e_neg = jnp.exp(-L[c])
        a_t.append(stack(a_s[rw, :] * jnp.exp(L[c] - lwc[c])).astype(bf16))
        r_t.append(stack(r_s[rw, :] * jnp.exp(L[c])))
        r_tb.append(r_t[c].astype(bf16))
        bk_t.append(jnp.concatenate([stack(b_s[rw, :] * e_neg), stack(k_s[rw, :] * e_neg)], axis=0).astype(bf16))
        v_st.append(stack(v_s[rw, :]).astype(bf16))

    A = [_dot_nt(jnp.concatenate([a_t[c], r_tb[c]], axis=0), bk_t[c]) for c in chunks]
    a_ab = [jnp.where(strict, x[0:PAIR, 0:PAIR], 0.0) for x in A]
    a_kk = [jnp.concatenate([jnp.where(strict, x[0:PAIR, PAIR:], 0.0),
                             jnp.where(incl, x[PAIR:, PAIR:], 0.0)], axis=0).astype(bf16) for x in A]
    a_rb = [jnp.where(incl, x[PAIR:, 0:PAIR], 0.0).astype(bf16) for x in A]

    p = a_ab
    tinv = [jnp.where(eye, 1.0, 0.0).astype(f32) + x for x in a_ab]
    for _ in range(5):
        pb = [x.astype(bf16) for x in p]
        p = [_dot(x, x) for x in pb]
        tinv = [tv + _dot(tv.astype(bf16), x.astype(bf16)) for tv, x in zip(tinv, p)]

    av = [_dot(a_kk[c], v_st[c]) for c in chunks]
    wu = [_dot(tinv[c].astype(bf16), jnp.concatenate([a_t[c], av[c][0:PAIR].astype(bf16)], axis=1))
          for c in chunks]
    wub = [x.astype(bf16) for x in wu]
    ry = [jnp.concatenate([r_t[c], av[c][PAIR:]], axis=1) + _dot(a_rb[c], wub[c]) for c in chunks]
    ry = [x[0:C] + x[C:] for x in ry]

    mct, gct = [], []
    for c in chunks:
        rw = rows[c]
        e_end = jnp.exp(LC[c] - L[c])
        b_h = stack(b_s[rw, :] * e_end).astype(bf16)
        k_h = stack(k_s[rw, :] * e_end).astype(bf16)
        mct.append(jnp.where(eye, jnp.exp(LC[c]), 0.0) + _dot_tn(b_h, wub[c][:, 0:PAIR]))
        gct.append(_dot_tn(jnp.concatenate([b_h, k_h], axis=0),
                           jnp.concatenate([wub[c][:, PAIR:], v_st[c]], axis=0)))

    st = st_ref[...]
    y = []
    for c in chunks:
        stb = st.astype(bf16)
        y.append(_dot(ry[c][:, 0:PAIR].astype(bf16), stb) + ry[c][:, PAIR:])
        st = _dot(mct[c].astype(bf16), stb) + gct[c]
    st_ref[...] = st

    mu = [_dot(x.astype(bf16), head_avg) for x in y]
    d = [x - m for x, m in zip(y, mu)]
    var = [_dot((x * x).astype(bf16), head_avg) for x in d]
    for c in chunks:
        yn = d[c] * lax.rsqrt(var[c] + GN_EPS) * lnw + lnb
        o_ref[rows[c], :] = ((yn + bon_s[rows[c], :]) * g_s[rows[c], :]).astype(o_ref.dtype)


def _rwkv(proj, B, S, mu_rkv, mu_lora, w0, a0, k_k, k_a, r_k, lnx_w, lnx_b, w_up_p, a_up_p, g_up, side,
          zero_shape):
    T = B * S
    tr = RWKV_TR
    nt = S // tr
    sub = tr // BF16_SUBLANES
    nsub = S // BF16_SUBLANES
    npair = RWKV_WIDTH // PAIR
    rcol, kcol, vcol = COL_R // PAIR, COL_KR // PAIR, COL_VR // PAIR
    locol = COL_LORA // LORA_PAD
    side_rows = side.shape[0] // (B * npair * nt)
    zero_rows = zero_shape[0] // (B * npair * nt)
    step = lambda b, h, t: ((b * npair + h) * nt + t, 0)

    def cur(col):
        return pl.BlockSpec((tr, PAIR), lambda b, h, t: (b * nt + t, col + h))

    def prev(col):
        return pl.BlockSpec((BF16_SUBLANES, PAIR),
                            lambda b, h, t: (b * nsub + jnp.maximum(t * sub - 1, 0), col + h))

    def vec(off):
        return pl.BlockSpec((1, PAIR), lambda b, h, t: (0, off + h))

    in_specs = [
        cur(rcol), cur(kcol), cur(vcol),
        pl.BlockSpec((tr, LORA_PAD), lambda b, h, t: (b * nt + t, locol)),
        prev(rcol), prev(kcol), prev(vcol),
        pl.BlockSpec((BF16_SUBLANES, LORA_PAD),
                     lambda b, h, t: (b * nsub + jnp.maximum(t * sub - 1, 0), locol)),
        vec(0), vec(npair), vec(2 * npair),
        pl.BlockSpec((1, LORA_PAD), lambda b, h, t: (0, 0)),
        vec(0), vec(0), vec(0), vec(0), vec(0), vec(0), vec(0),
        pl.BlockSpec((LANES, PAIR), lambda b, h, t: (0, h)),
        pl.BlockSpec((LANES, PAIR), lambda b, h, t: (0, h)),
        pl.BlockSpec((GATE_LORA, PAIR), lambda b, h, t: (0, h)),
        pl.BlockSpec((side_rows, side.shape[1]), step),
    ]
    scratch = [pltpu.VMEM((PAIR, PAIR), f32)] + [pltpu.VMEM((tr, PAIR), f32)] * 8
    return pl.pallas_call(
        _rwkv_kernel,
        out_shape=(jax.ShapeDtypeStruct((T, RWKV_WIDTH), bf16),
                   jax.ShapeDtypeStruct(side.shape, bf16),
                   jax.ShapeDtypeStruct(zero_shape, f32)),
        grid=(B, npair, nt),
        in_specs=in_specs,
        out_specs=(pl.BlockSpec((tr, PAIR), lambda b, h, t: (b * nt + t, h)),
                   pl.BlockSpec((side_rows, side.shape[1]), step),
                   pl.BlockSpec((zero_rows, zero_shape[1]), step)),
        scratch_shapes=scratch,
        compiler_params=pltpu.CompilerParams(
            dimension_semantics=("parallel", "parallel", "arbitrary"), vmem_limit_bytes=VMEM_LIMIT),
        name="rwkv7",
    )(proj, proj, proj, proj, proj, proj, proj, proj,
      mu_rkv, mu_rkv, mu_rkv, mu_lora, w0, a0, k_k, k_a, r_k, lnx_w, lnx_b, w_up_p, a_up_p, g_up, side)


def _layer_norm(z, w, b):
    mu = jnp.mean(z, axis=-1, keepdims=True)
    d = z - mu
    var = jnp.mean(d * d, axis=-1, keepdims=True)
    return d * lax.rsqrt(var + LN_EPS) * w + b


def _merge_kernel(x_ref, ya_ref, yb_ref, ga_ref, gb_ref, pa_ref, pb_ref, wo_ref, l1w_ref, l1b_ref,
                  wr_ref, br_ref,
                  h_ref, topi_ref, gate_ref, rank_ref, cnt_ref,
                  run_ref):
    i = pl.program_id(0)
    tm = x_ref.shape[0]
    ne = wr_ref.shape[1]

    @pl.when(i == 0)
    def _():
        run_ref[...] = jnp.zeros_like(run_ref)

    ma = _dot(ya_ref[...], pa_ref[...])
    mb = _dot(yb_ref[...], pb_ref[...])
    merged = (jax.nn.sigmoid(ga_ref[...].astype(f32)) * ma
              + jax.nn.sigmoid(gb_ref[...].astype(f32)) * mb)
    mix = _dot(merged.astype(bf16), wo_ref[...])
    h = _layer_norm(DEEPNORM_ALPHA * x_ref[...] + mix, l1w_ref[...], l1b_ref[...])
    h_ref[...] = h

    h_hi, h_lo = _split_bf16(h)
    w_hi, w_lo = _split_bf16(wr_ref[...])
    logits = _dot(h_hi, w_hi) + (_dot(h_lo, w_hi) + _dot(h_hi, w_lo)) + br_ref[...]
    eidx = lax.broadcasted_iota(i32, (tm, ne), 1)
    slot = lax.broadcasted_iota(i32, (tm, LANES), 1)
    lg = logits
    vals, idxs = [], []
    for _ in range(TOP_K):
        m = jnp.max(lg, axis=-1, keepdims=True)
        idx = jnp.min(jnp.where(lg == m, eidx, ne), axis=-1, keepdims=True)
        vals.append(m)
        idxs.append(idx)
        lg = jnp.where(eidx == idx, -jnp.inf, lg)
    exps = [jnp.exp(vk - vals[0]) for vk in vals]
    denom = exps[0] + exps[1] + exps[2] + exps[3]

    onehots = [jnp.where(eidx == idx, 1.0, 0.0).astype(f32) for idx in idxs]
    oh = onehots[0] + onehots[1] + onehots[2] + onehots[3]
    tri = jnp.where(lax.broadcasted_iota(i32, (tm, tm), 1) < lax.broadcasted_iota(i32, (tm, tm), 0),
                    1.0, 0.0).astype(bf16)
    before = run_ref[...] + _dot(tri, oh.astype(bf16))

    topi_o = jnp.zeros((tm, LANES), i32)
    gate_o = jnp.zeros((tm, LANES), f32)
    rank_o = jnp.zeros((tm, LANES), i32)
    for kslot in range(TOP_K):
        rk = jnp.sum(onehots[kslot] * before, axis=-1, keepdims=True).astype(i32)
        topi_o = jnp.where(slot == kslot, idxs[kslot], topi_o)
        gate_o = jnp.where(slot == kslot, exps[kslot] / denom, gate_o)
        rank_o = jnp.where(slot == kslot, rk, rank_o)
    topi_ref[...] = topi_o
    gate_ref[...] = gate_o
    rank_ref[...] = rank_o
    run = run_ref[...] + jnp.sum(oh, axis=0, keepdims=True)
    run_ref[...] = run
    cnt_ref[...] = run


def _merge_router(x2, ya, yb, proj, pa, pb, wo, l1w, l1b, wr, br):
    T, D = x2.shape
    tm = MERGE_TM
    ne = wr.shape[1]
    gcol = COL_GATE // D

    def const(shape):
        return pl.BlockSpec(shape, lambda i: (0, 0), pipeline_mode=pl.Buffered(1))

    out_shape = (jax.ShapeDtypeStruct((T, D), f32),
                 jax.ShapeDtypeStruct((T, LANES), i32),
                 jax.ShapeDtypeStruct((T, LANES), f32),
                 jax.ShapeDtypeStruct((T, LANES), i32),
                 jax.ShapeDtypeStruct((1, ne), f32))
    return pl.pallas_call(
        _merge_kernel,
        out_shape=out_shape,
        grid=(T // tm,),
        in_specs=[pl.BlockSpec((tm, D), lambda i: (i, 0)),
                  pl.BlockSpec((tm, ATT_WIDTH), lambda i: (i, 0)),
                  pl.BlockSpec((tm, RWKV_WIDTH), lambda i: (i, 0)),
                  pl.BlockSpec((tm, D), lambda i: (i, gcol)),
                  pl.BlockSpec((tm, D), lambda i: (i, gcol + 1)),
                  const(pa.shape), const(pb.shape), const(wo.shape),
                  const((1, D)), const((1, D)), const(wr.shape), const((1, ne))],
        out_specs=(pl.BlockSpec((tm, D), lambda i: (i, 0)),
                   pl.BlockSpec((tm, LANES), lambda i: (i, 0)),
                   pl.BlockSpec((tm, LANES), lambda i: (i, 0)),
                   pl.BlockSpec((tm, LANES), lambda i: (i, 0)),
                   pl.BlockSpec((1, ne), lambda i: (0, 0))),
        scratch_shapes=[pltpu.VMEM((1, ne), f32)],
        compiler_params=pltpu.CompilerParams(
            dimension_semantics=("arbitrary",), vmem_limit_bytes=VMEM_LIMIT),
        name="merge_ln1_router",
    )(x2, ya, yb, proj, proj, pa, pb, wo, l1w, l1b, wr, br)


def _dispatch_kernel(dest_ref, h_ref, xs_in_hbm, xs_hbm, sem):
    del xs_in_hbm
    tm = h_ref.shape[0]

    def row_copy(t, k):
        return pltpu.make_async_copy(h_ref.at[pl.ds(t, 1)],
                                     xs_hbm.at[pl.ds(dest_ref[t * TOP_K + k], 1)], sem)

    def start(t, c):
        for k in range(TOP_K):
            row_copy(t, k).start()
        return c

    def wait(t, c):
        for k in range(TOP_K):
            row_copy(t, k).wait()
        return c

    lax.fori_loop(0, tm, start, 0, unroll=DMA_UNROLL)
    lax.fori_loop(0, tm, wait, 0, unroll=DMA_UNROLL)


def _dispatch(dest_flat, h, xs0):
    T, D = h.shape
    n = COMBINE_TM * TOP_K
    return pl.pallas_call(
        _dispatch_kernel,
        out_shape=jax.ShapeDtypeStruct(xs0.shape, xs0.dtype),
        grid=(T * TOP_K // n,),
        in_specs=[pl.BlockSpec((n,), lambda i: (i,), memory_space=pltpu.SMEM),
                  pl.BlockSpec((n // TOP_K, D), lambda i: (i, 0)),
                  pl.BlockSpec(memory_space=pl.ANY)],
        out_specs=pl.BlockSpec(memory_space=pl.ANY),
        scratch_shapes=[pltpu.SemaphoreType.DMA(())],
        input_output_aliases={2: 0},
        compiler_params=pltpu.CompilerParams(
            dimension_semantics=("arbitrary",), has_side_effects=True, vmem_limit_bytes=VMEM_LIMIT),
        name="moe_dispatch",
    )(dest_flat, h, xs0)


def _expert_kernel(be_ref, nu_ref, nv_ref, x_ref, wg_ref, wu_ref, bg_ref, bu_ref, wd_ref, bd_ref, o_ref, xb_ref):
    del be_ref
    i = pl.program_id(0)
    j = pl.program_id(1)
    used = i < nu_ref[0]
    half = o_ref.shape[0] // 2
    top_only = nv_ref[i] <= half

    @pl.when(jnp.logical_and(used, j == 0))
    def _():
        xb_ref[...] = x_ref[...].astype(bf16)
        o_ref[...] = jnp.broadcast_to(bd_ref[0], o_ref.shape)

    @pl.when(jnp.logical_and(jnp.logical_not(used), j == 0))
    def _():
        o_ref[...] = jnp.zeros_like(o_ref)

    def ffn(rows):
        xb = xb_ref[rows, :]
        g = _dot(xb, wg_ref[0]) + bg_ref[0]
        u = _dot(xb, wu_ref[0]) + bu_ref[0]
        g = jnp.minimum(g, SWIGLU_LIMIT)
        u = jnp.clip(u, -SWIGLU_LIMIT, SWIGLU_LIMIT)
        act = (u + 1.0) * (g * jax.nn.sigmoid(SWIGLU_ALPHA * g))
        o_ref[rows, :] += _dot(act.astype(bf16), wd_ref[0])

    @pl.when(jnp.logical_and(used, jnp.logical_not(top_only)))
    def _():
        ffn(slice(None))

    @pl.when(jnp.logical_and(used, top_only))
    def _():
        ffn(slice(0, half))


def _experts(blk_e, nused, nvalid, xs, w_gu, b_gu, w_down, b_down):
    R, D = xs.shape
    E, _, F2 = w_gu.shape
    F = F2 // 2
    rb, tf = MOE_RB, min(MOE_TF, F)
    nf = F // tf

    def jj(i, j, nu):
        return jnp.where(i < nu[0], j, 0)

    grid_spec = pltpu.PrefetchScalarGridSpec(
        num_scalar_prefetch=3,
        grid=(R // rb, nf),
        in_specs=[pl.BlockSpec((rb, D), lambda i, j, be, nu, nv: (jnp.where(i < nu[0], i, 0), 0)),
                  pl.BlockSpec((1, D, tf), lambda i, j, be, nu, nv: (be[i], 0, jj(i, j, nu))),
                  pl.BlockSpec((1, D, tf), lambda i, j, be, nu, nv: (be[i], 0, nf + jj(i, j, nu))),
                  pl.BlockSpec((1, 1, tf), lambda i, j, be, nu, nv: (be[i], 0, jj(i, j, nu))),
                  pl.BlockSpec((1, 1, tf), lambda i, j, be, nu, nv: (be[i], 0, nf + jj(i, j, nu))),
                  pl.BlockSpec((1, tf, D), lambda i, j, be, nu, nv: (be[i], jj(i, j, nu), 0)),
                  pl.BlockSpec((1, 1, D), lambda i, j, be, nu, nv: (be[i], 0, 0))],
        out_specs=pl.BlockSpec((rb, D), lambda i, j, be, nu, nv: (i, 0)),
        scratch_shapes=[pltpu.VMEM((rb, D), bf16)],
    )
    return pl.pallas_call(
        _expert_kernel,
        out_shape=jax.ShapeDtypeStruct((R, D), f32),
        grid_spec=grid_spec,
        compiler_params=pltpu.CompilerParams(
            dimension_semantics=("parallel", "arbitrary"), vmem_limit_bytes=VMEM_LIMIT),
        name="moe_experts",
    )(blk_e, nused, nvalid, xs, w_gu, w_gu, b_gu, b_gu, w_down, b_down)


def _combine_kernel(dest_ref, h_ref, gate_ref, l2w_ref, l2b_ref, y_hbm, o_ref, buf, sem):
    tm = h_ref.shape[0]

    def row_copy(t, k):
        return pltpu.make_async_copy(y_hbm.at[pl.ds(dest_ref[t * TOP_K + k], 1)],
                                     buf.at[k, pl.ds(t, 1)], sem)

    def start(t, c):
        for k in range(TOP_K):
            row_copy(t, k).start()
        return c

    def wait(t, c):
        for k in range(TOP_K):
            row_copy(t, k).wait()
        return c

    lax.fori_loop(0, tm, start, 0, unroll=DMA_UNROLL)
    lax.fori_loop(0, tm, wait, 0, unroll=DMA_UNROLL)

    gate = gate_ref[...]
    ffn = gate[:, 0:1] * buf[0]
    for kslot in range(1, TOP_K):
        ffn = ffn + gate[:, kslot:kslot + 1] * buf[kslot]
    o_ref[...] = _layer_norm(DEEPNORM_ALPHA * h_ref[...] + ffn, l2w_ref[...], l2b_ref[...])


def _combine(dest_flat, h, gate, l2w, l2b, y):
    T, D = h.shape
    tm = COMBINE_TM
    n = tm * TOP_K
    return pl.pallas_call(
        _combine_kernel,
        out_shape=jax.ShapeDtypeStruct((T, D), f32),
        grid=(T // tm,),
        in_specs=[pl.BlockSpec((n,), lambda i: (i,), memory_space=pltpu.SMEM),
                  pl.BlockSpec((tm, D), lambda i: (i, 0)),
                  pl.BlockSpec((tm, LANES), lambda i: (i, 0)),
                  pl.BlockSpec((1, D), lambda i: (0, 0)),
                  pl.BlockSpec((1, D), lambda i: (0, 0)),
                  pl.BlockSpec(memory_space=pl.ANY)],
        out_specs=pl.BlockSpec((tm, D), lambda i: (i, 0)),
        scratch_shapes=[pltpu.VMEM((TOP_K, tm, D), f32), pltpu.SemaphoreType.DMA(())],
        compiler_params=pltpu.CompilerParams(
            dimension_semantics=("arbitrary",), vmem_limit_bytes=VMEM_LIMIT),
        name="moe_combine_ln2",
    )(dest_flat, h, gate, l2w, l2b, y)


def _pack_w_in(w_in):
    o1 = 3 * ATT_WIDTH
    o2 = o1 + 3 * RWKV_WIDTH
    o3 = o2 + DECAY_LORA
    o4 = o3 + ICLR_LORA
    o5 = o4 + GATE_LORA
    padc = lambda w, n: jnp.pad(w, ((0, 0), (0, n - w.shape[1])))
    parts = [w_in[:, o5:], w_in[:, :o2], padc(w_in[:, o2:o3], LANES), padc(w_in[:, o3:o4], LANES),
             w_in[:, o4:o5]]
    return jnp.concatenate(parts, axis=1).astype(bf16)


def _token_mixing(x2, B, S, w_in, rel_bias, shift_mu, w0, w_up, a0, a_up, g_up, k_k, k_a, r_k,
                  lnx_w, lnx_b, side_a, side_b, zero_shape):
    T, D = x2.shape
    wp = _pack_w_in(w_in)
    proj = _in_proj(x2, wp, min(1024, T), 1536)

    y_a, side_a = _attention(proj, _attn_bias_table(rel_bias), B, S, side_a)

    W = RWKV_WIDTH
    row = lambda p: p.reshape(1, -1).astype(f32)
    padv = lambda p, n: jnp.pad(p, (0, n - p.shape[0]))
    mu_rkv = row(shift_mu[:3 * W])
    mu_lora = row(jnp.concatenate([padv(shift_mu[3 * W:3 * W + DECAY_LORA], LANES),
                                   padv(shift_mu[3 * W + DECAY_LORA:3 * W + DECAY_LORA + ICLR_LORA], LANES),
                                   shift_mu[3 * W + DECAY_LORA + ICLR_LORA:]]))
    padr = lambda w: jnp.pad(w, ((0, LANES - w.shape[0]), (0, 0))).astype(bf16)
    y_b, side_b, zeros = _rwkv(proj, B, S, mu_rkv, mu_lora, row(w0), row(a0), row(k_k), row(k_a), row(r_k),
                               row(lnx_w), row(lnx_b), padr(w_up), padr(a_up), g_up.astype(bf16), side_b,
                               zero_shape)
    return proj, y_a, y_b, side_a, side_b, zeros


def _moe_routing(topi, rank, counts, n_experts, rb, n_blocks):
    M = topi.shape[0] * TOP_K
    counts = counts.astype(i32)
    padded = (counts + rb - 1) // rb * rb
    pad_ends = jnp.cumsum(padded)
    pad_starts = pad_ends - padded
    onehot = topi[..., None] == jnp.arange(n_experts, dtype=i32)
    dest = (jnp.sum(jnp.where(onehot, pad_starts, 0), axis=-1) + rank).reshape(M)
    blk_start = jnp.arange(n_blocks, dtype=i32)[:, None] * rb
    blk_e = jnp.minimum(jnp.sum((pad_ends[None, :] <= blk_start).astype(i32), axis=-1), n_experts - 1)
    nused = (pad_ends[-1:] // rb).astype(i32)
    nvalid = jnp.clip((pad_starts + counts)[blk_e] - blk_start[:, 0], 0, rb).astype(i32)
    return dest.astype(i32), blk_e, nused, nvalid


def kernel(x, w_in, rel_bias, shift_mu, w0, w_up, a0, a_up, g_up, k_k, k_a, r_k, lnx_w, lnx_b,
           proj_a, proj_b, w_out, ln1_w, ln1_b, w_router, b_router, w_gu, b_gu, w_down, b_down,
           ln2_w, ln2_b):
    B, S, D = x.shape
    T = B * S
    h = x.reshape(T, D)
    for l in range(DEPTH):
        E, F = w_down.shape[1], w_down.shape[2]
        n_blocks = (T * TOP_K + E * (MOE_RB - 1) + MOE_RB - 1) // MOE_RB
        proj, y_a, y_b, w_down_b, w_gu_b, xs0 = _token_mixing(
            h, B, S, w_in[l], rel_bias[l], shift_mu[l], w0[l], w_up[l], a0[l], a_up[l], g_up[l],
            k_k[l], k_a[l], r_k[l], lnx_w[l], lnx_b[l],
            w_down[l].reshape(E * F, D), w_gu[l].reshape(E * D, 2 * F), (n_blocks * MOE_RB, D))
        h1, topi, gate, rank, counts = _merge_router(
            h, y_a, y_b, proj, proj_a[l].astype(bf16), proj_b[l].astype(bf16), w_out[l].astype(bf16),
            ln1_w[l].reshape(1, D), ln1_b[l].reshape(1, D), w_router[l], b_router[l].reshape(1, E))
        dest, blk_e, nused, nvalid = _moe_routing(
            topi[:, :TOP_K], rank[:, :TOP_K], counts[0], E, MOE_RB, n_blocks)
        xs = _dispatch(dest, h1, xs0)
        y = _experts(blk_e, nused, nvalid, xs, w_gu_b.reshape(E, D, 2 * F), b_gu[l].reshape(E, 1, -1),
                     w_down_b.reshape(E, F, D), b_down[l].reshape(E, 1, D))
        h = _combine(dest, h1, gate, ln2_w[l].reshape(1, D), ln2_b[l].reshape(1, D), y)
    return h.reshape(B, S, D)
```

```python
import jax
import jax.numpy as jnp
from jax import lax
from jax.experimental import pallas as pl
from jax.experimental.pallas import tpu as pltpu

f32 = jnp.float32
bf16 = jnp.bfloat16
i32 = jnp.int32

CHUNK = 64
LEFT_CHUNKS = 8
HEAD_DIM = 64
ATT_WIDTH = 1024
MAX_PAST_DIST = 256
REL_TABLE = MAX_PAST_DIST + CHUNK
RWKV_WIDTH = 1024
DECAY_LORA = 96
ICLR_LORA = 96
GATE_LORA = 256
TOP_K = 4
SWIGLU_LIMIT = 7.0
SWIGLU_ALPHA = 1.702
LN_EPS = 1e-5
GN_EPS = 64e-5
DEPTH = 1
DEEPNORM_ALPHA = (2 * DEPTH) ** 0.25

LANES = 128
PAIR = 2 * HEAD_DIM
F32_SUBLANES = 8
BF16_SUBLANES = 16
VMEM_LIMIT = 56 * 1024 * 1024

NEG = -1e30

LORA_PAD = 512
COL_GATE = 0
COL_Q = 4096
COL_K = COL_Q + ATT_WIDTH
COL_V = COL_K + ATT_WIDTH
COL_R = COL_V + ATT_WIDTH
COL_KR = COL_R + RWKV_WIDTH
COL_VR = COL_KR + RWKV_WIDTH
COL_LORA = COL_VR + RWKV_WIDTH
PROJ_COLS = COL_LORA + LORA_PAD

ATT_TQ = 256
ATT_HB = 4
RWKV_TR = 1024
LORA_TR = 1024
MERGE_TM = 256
MOE_RB = 512
MOE_TF = 1024
COMBINE_TM = 256
DMA_UNROLL = 4


def _dot(a, b, precision=None):
    return jnp.dot(a, b, preferred_element_type=f32, precision=precision)


def _dot_nt(a, b):
    return lax.dot_general(a, b, (((1,), (1,)), ((), ())), preferred_element_type=f32)


def _dot_tn(a, b):
    return lax.dot_general(a, b, (((0,), (0,)), ((), ())), preferred_element_type=f32)


def _split_bf16(a):
    hi = a.astype(bf16)
    return hi, (a - hi.astype(f32)).astype(bf16)


def _dot_split_lhs(a, b_bf16):
    hi, lo = _split_bf16(a)
    return _dot(hi, b_bf16) + _dot(lo, b_bf16)


def _dot_split_rhs(a_bf16, b):
    hi, lo = _split_bf16(b)
    return _dot(a_bf16, hi) + _dot(a_bf16, lo)


def _inproj_kernel(x_ref, w_ref, o_ref, xb_ref):
    @pl.when(pl.program_id(1) == 0)
    def _():
        xb_ref[...] = x_ref[...].astype(bf16)

    o_ref[...] = _dot(xb_ref[...], w_ref[...]).astype(o_ref.dtype)


def _in_proj(x2, w, tm, tn):
    M, K = x2.shape
    N = w.shape[1]
    return pl.pallas_call(
        _inproj_kernel,
        out_shape=jax.ShapeDtypeStruct((M, N), bf16),
        grid=(M // tm, N // tn),
        in_specs=[pl.BlockSpec((tm, K), lambda i, j: (i, 0)),
                  pl.BlockSpec((K, tn), lambda i, j: (0, j))],
        out_specs=pl.BlockSpec((tm, tn), lambda i, j: (i, j)),
        scratch_shapes=[pltpu.VMEM((tm, K), bf16)],
        compiler_params=pltpu.CompilerParams(
            dimension_semantics=("parallel", "arbitrary"), vmem_limit_bytes=VMEM_LIMIT),
        name="in_proj",
    )(x2, w)


def _attn_bias_table(rel_bias):
    tq = ATT_TQ
    nk = 3 * tq
    H = rel_bias.shape[0]
    C = CHUNK
    nqc, nkc = tq // C, nk // C
    e = jnp.arange(-(nkc - 1), nqc)
    dist = (e[:, None] * C + jnp.arange(-(C - 1), C)[None, :]) + 2 * tq
    idx = jnp.clip(jnp.minimum(dist, MAX_PAST_DIST) + (C - 1), 0, REL_TABLE - 1)
    w = jnp.pad(rel_bias.astype(f32)[:, idx], ((0, 0), (0, 0), (0, 1)))
    ne = e.shape[0]
    G = jnp.broadcast_to(w[:, :, None, :], (H, ne, C, 2 * C)).reshape(H, ne, C * 2 * C)
    G = G[:, :, :C * (2 * C - 1)].reshape(H, ne, C, 2 * C - 1)
    blocks = jnp.swapaxes(G[:, :, :, C - 1:], 2, 3)
    qc = jnp.arange(nqc)[:, None]
    kc = jnp.arange(nkc)[None, :]
    d = qc + (2 * tq) // C - kc
    tiles = blocks[:, (qc - kc) + (nkc - 1)]
    tiles = jnp.where(((d >= 0) & (d <= LEFT_CHUNKS))[None, :, :, None, None], tiles, NEG)
    return jnp.transpose(tiles, (0, 1, 3, 2, 4)).reshape(H, tq, nk)


def _attn_kernel(q_ref, k0_ref, k1_ref, k2_ref, v0_ref, v1_ref, v2_ref, bias_ref, side_ref,
                 o_ref, side_o_ref):
    qt = pl.program_id(2)
    side_o_ref[...] = side_ref[...].astype(side_o_ref.dtype)
    tq = q_ref.shape[0]
    k_refs = (k0_ref, k1_ref, k2_ref)
    v_refs = (v0_ref, v1_ref, v2_ref)
    heads = range(ATT_HB)
    blocks = range(3)
    lane = lax.broadcasted_iota(i32, (1, PAIR), 1)
    head0 = lane < HEAD_DIM
    pair = [slice((h // 2) * PAIR, (h // 2 + 1) * PAIR) for h in heads]
    q = [jnp.where(head0 if h % 2 == 0 else jnp.logical_not(head0),
                   q_ref[:, pair[h]] * (HEAD_DIM ** -0.5), 0.0).astype(bf16) for h in heads]
    s = [[_dot_nt(q[h], k_refs[j][:, pair[h]]) + bias_ref[h, :, j * tq:(j + 1) * tq] for j in blocks]
         for h in heads]
    s = [[jnp.where(qt - 2 + j >= 0, s[h][j], NEG) if j < 2 else s[h][j] for j in blocks] for h in heads]
    m = [jnp.max(jnp.maximum(jnp.maximum(s[h][0], s[h][1]), s[h][2]), axis=-1, keepdims=True)
         for h in heads]
    p = [[jnp.exp((s[h][j] - m[h]).astype(bf16)) for j in blocks] for h in heads]
    ones = jnp.ones((tq, PAIR), bf16)
    v1 = [[jnp.concatenate([v_refs[j][:, pair[2 * g]], ones], axis=1) for j in blocks]
          for g in range(ATT_HB // 2)]
    acc = [_dot(p[h][0], v1[h // 2][0]) + _dot(p[h][1], v1[h // 2][1]) + _dot(p[h][2], v1[h // 2][2])
           for h in heads]
    out = [acc[h][:, 0:PAIR] / acc[h][:, PAIR:PAIR + 1] for h in heads]
    for g in range(ATT_HB // 2):
        o_ref[:, pair[2 * g]] = jnp.where(head0, out[2 * g], out[2 * g + 1]).astype(o_ref.dtype)


def _attention(proj, bias, B, S, side):
    T = B * S
    tq = ATT_TQ
    nt = S // tq
    width = ATT_HB * HEAD_DIM
    ngrp = ATT_WIDTH // width
    side_spec = pl.BlockSpec((side.shape[0] // (B * nt * ngrp), side.shape[1]),
                             lambda h, b, t: ((h * B + b) * nt + t, 0))
    qcol, kcol, vcol = COL_Q // width, COL_K // width, COL_V // width

    def kv_spec(col, back):
        return pl.BlockSpec((tq, width),
                            lambda h, b, t: (b * nt + jnp.maximum(t - back, 0), col + h))

    return pl.pallas_call(
        _attn_kernel,
        out_shape=(jax.ShapeDtypeStruct((T, ATT_WIDTH), bf16), jax.ShapeDtypeStruct(side.shape, bf16)),
        grid=(ngrp, B, nt),
        in_specs=[pl.BlockSpec((tq, width), lambda h, b, t: (b * nt + t, qcol + h)),
                  kv_spec(kcol, 2), kv_spec(kcol, 1), kv_spec(kcol, 0),
                  kv_spec(vcol, 2), kv_spec(vcol, 1), kv_spec(vcol, 0),
                  pl.BlockSpec((ATT_HB, tq, 3 * tq), lambda h, b, t: (h, 0, 0)),
                  side_spec],
        out_specs=(pl.BlockSpec((tq, width), lambda h, b, t: (b * nt + t, h)), side_spec),
        compiler_params=pltpu.CompilerParams(
            dimension_semantics=("parallel", "parallel", "parallel"), vmem_limit_bytes=VMEM_LIMIT),
        name="chunk_attn",
    )(proj, proj, proj, proj, proj, proj, proj, bias, side)


def _lora_prep_kernel(lo_ref, lop_ref, mu_ref, o_ref):
    t = pl.program_id(1)
    tr = lo_ref.shape[0]
    p = lo_ref[...].astype(f32)
    prev_row = lop_ref[BF16_SUBLANES - 1:BF16_SUBLANES, :].astype(f32) * jnp.where(t == 0, 0.0, 1.0)
    row = lax.broadcasted_iota(i32, (tr, 1), 0)
    prev = jnp.where(row == 0, prev_row, pltpu.roll(p, 1, 0))
    lo = p + (prev - p) * mu_ref[...]
    o_ref[:, 0:LANES] = jnp.tanh(lo[:, 0:LANES]).astype(o_ref.dtype)
    o_ref[:, LANES:2 * LANES] = lo[:, LANES:2 * LANES].astype(o_ref.dtype)
    o_ref[:, 2 * LANES:] = jax.nn.sigmoid(lo[:, 2 * LANES:]).astype(o_ref.dtype)


def _lora_prep(proj, B, S, mu_lora):
    tr = LORA_TR
    nt = S // tr
    sub = tr // BF16_SUBLANES
    nsub = S // BF16_SUBLANES
    locol = COL_LORA // LORA_PAD
    return pl.pallas_call(
        _lora_prep_kernel,
        out_shape=jax.ShapeDtypeStruct((B * S, LORA_PAD), bf16),
        grid=(B, nt),
        in_specs=[pl.BlockSpec((tr, LORA_PAD), lambda b, t: (b * nt + t, locol)),
                  pl.BlockSpec((BF16_SUBLANES, LORA_PAD),
                               lambda b, t: (b * nsub + jnp.maximum(t * sub - 1, 0), locol)),
                  pl.BlockSpec((1, LORA_PAD), lambda b, t: (0, 0))],
        out_specs=pl.BlockSpec((tr, LORA_PAD), lambda b, t: (b * nt + t, 0)),
        compiler_params=pltpu.CompilerParams(
            dimension_semantics=("parallel", "parallel"), vmem_limit_bytes=VMEM_LIMIT),
        name="rwkv7_lora_prep",
    )(proj, proj, mu_lora)


def _rwkv_kernel(r_ref, k_ref, v_ref, act_ref, rp_ref, kp_ref, vp_ref,
                 mur_ref, muk_ref, muv_ref, w0_ref, a0_ref, kkp_ref, kap_ref, rkp_ref,
                 lnw_ref, lnb_ref, wup_ref, aup_ref, gup_ref, side_ref,
                 o_ref, side_o_ref, zero_o_ref,
                 st_ref, r_s, lw_s, k_s, v_s, a_s, b_s, g_s, bon_s):
    t = pl.program_id(2)
    tr = r_ref.shape[0]
    C = CHUNK

    side_o_ref[...] = side_ref[...].astype(side_o_ref.dtype)
    zero_o_ref[...] = jnp.zeros_like(zero_o_ref)

    @pl.when(t == 0)
    def _():
        st_ref[...] = jnp.zeros_like(st_ref)

    not_first = jnp.where(t == 0, 0.0, 1.0).astype(f32)
    row = lax.broadcasted_iota(i32, (tr, 1), 0)

    def shift(p_ref, prev_ref, mu_ref):
        p = p_ref[...].astype(f32)
        prev_row = prev_ref[BF16_SUBLANES - 1:BF16_SUBLANES, :].astype(f32) * not_first
        prev = jnp.where(row == 0, prev_row, pltpu.roll(p, 1, 0))
        return p + (prev - p) * mu_ref[...]

    r = shift(r_ref, rp_ref, mur_ref)
    k = shift(k_ref, kp_ref, muk_ref)
    v = shift(v_ref, vp_ref, muv_ref)

    ri = lax.broadcasted_iota(i32, (PAIR, PAIR), 0)
    ci = lax.broadcasted_iota(i32, (PAIR, PAIR), 1)
    same_head = (ri // HEAD_DIM) == (ci // HEAD_DIM)
    head_ones = jnp.where(same_head, 1.0, 0.0).astype(bf16)
    head_avg = jnp.where(same_head, 1.0 / HEAD_DIM, 0.0).astype(bf16)

    z = w0_ref[...] + _dot(act_ref[:, 0:LANES], wup_ref[...])
    softplus_negz = jnp.maximum(-z, 0.0) + jnp.log(1.0 + jnp.exp(-jnp.abs(z)))
    lw = -jnp.exp(-softplus_negz - 0.5)
    alr = jax.nn.sigmoid(a0_ref[...] + _dot(act_ref[:, LANES:2 * LANES], aup_ref[...]))
    g = _dot(act_ref[:, 2 * LANES:], gup_ref[...])
    kk = k * kkp_ref[...]
    kk = kk / jnp.maximum(jnp.sqrt(_dot_split_lhs(kk * kk, head_ones)), 1e-12)
    k2 = k * (1.0 + (alr - 1.0) * kap_ref[...])
    bonus = _dot_split_lhs(r * k2 * rkp_ref[...], head_ones) * v

    r_s[...] = r
    lw_s[...] = lw
    k_s[...] = k2
    v_s[...] = v
    a_s[...] = -kk
    b_s[...] = kk * alr
    g_s[...] = g
    bon_s[...] = bonus

    lane = lax.broadcasted_iota(i32, (1, PAIR), 1)
    head0 = lane < HEAD_DIM
    tri_incl = jnp.where(lax.broadcasted_iota(i32, (C, C), 1) <= lax.broadcasted_iota(i32, (C, C), 0),
                         1.0, 0.0).astype(bf16)
    strict = ci < ri
    incl = ci <= ri
    eye = ci == ri
    lnw = lnw_ref[...]
    lnb = lnb_ref[...]

    def stack(x):
        return jnp.concatenate([jnp.where(head0, x, 0.0), jnp.where(head0, 0.0, x)], axis=0)

    nc = tr // C
    chunks = range(nc)
    rows = [slice(c * C, (c + 1) * C) for c in chunks]
    lwc = [lw_s[rw, :] for rw in rows]
    L = [_dot_split_rhs(tri_incl, x) for x in lwc]
    LC = [x[C - 1:C, :] for x in L]
    a_t, r_t, r_tb, bk_t, v_st = [], [], [], [], []
    for c in chunks:
        rw = rows[c]
        e_neg = jnp.exp(-L[c])
        a_t.append(stack(a_s[rw, :] * jnp.exp(L[c] - lwc[c])).astype(bf16))
        r_t.append(stack(r_s[rw, :] * jnp.exp(L[c])))
        r_tb.append(r_t[c].astype(bf16))
        bk_t.append(jnp.concatenate([stack(b_s[rw, :] * e_neg), stack(k_s[rw, :] * e_neg)], axis=0).astype(bf16))
        v_st.append(stack(v_s[rw, :]).astype(bf16))

    A = [_dot_nt(jnp.concatenate([a_t[c], r_tb[c]], axis=0), bk_t[c]) for c in chunks]
    a_ab = [jnp.where(strict, x[0:PAIR, 0:PAIR], 0.0) for x in A]
    a_kk = [jnp.concatenate([jnp.where(strict, x[0:PAIR, PAIR:], 0.0),
                             jnp.where(incl, x[PAIR:, PAIR:], 0.0)], axis=0).astype(bf16) for x in A]
    a_rb = [jnp.where(incl, x[PAIR:, 0:PAIR], 0.0).astype(bf16) for x in A]

    p = a_ab
    tinv = [jnp.where(eye, 1.0, 0.0).astype(f32) + x for x in a_ab]
    for _ in range(5):
        pb = [x.astype(bf16) for x in p]
        p = [_dot(x, x) for x in pb]
        tinv = [tv + _dot(tv.astype(bf16), x.astype(bf16)) for tv, x in zip(tinv, p)]

    av = [_dot(a_kk[c], v_st[c]) for c in chunks]
    wu = [_dot(tinv[c].astype(bf16), jnp.concatenate([a_t[c], av[c][0:PAIR].astype(bf16)], axis=1))
          for c in chunks]
    wub = [x.astype(bf16) for x in wu]
    ry = [jnp.concatenate([r_t[c], av[c][PAIR:]], axis=1) + _dot(a_rb[c], wub[c]) for c in chunks]
    ry = [x[0:C] + x[C:] for x in ry]

    mct, gct = [], []
    for c in chunks:
        rw = rows[c]
        e_end = jnp.exp(LC[c] - L[c])
        b_h = stack(b_s[rw, :] * e_end).astype(bf16)
        k_h = stack(k_s[rw, :] * e_end).astype(bf16)
        mct.append(jnp.where(eye, jnp.exp(LC[c]), 0.0) + _dot_tn(b_h, wub[c][:, 0:PAIR]))
        gct.append(_dot_tn(jnp.concatenate([b_h, k_h], axis=0),
                           jnp.concatenate([wub[c][:, PAIR:], v_st[c]], axis=0)))

    st = st_ref[...]
    y = []
    for c in chunks:
        stb = st.astype(bf16)
        y.append(_dot(ry[c][:, 0:PAIR].astype(bf16), stb) + ry[c][:, PAIR:])
        st = _dot(mct[c].astype(bf16), stb) + gct[c]
    st_ref[...] = st

    mu = [_dot(x.astype(bf16), head_avg) for x in y]
    d = [x - m for x, m in zip(y, mu)]
    var = [_dot((x * x).astype(bf16), head_avg) for x in d]
    for c in chunks:
        yn = d[c] * lax.rsqrt(var[c] + GN_EPS) * lnw + lnb
        o_ref[rows[c], :] = ((yn + bon_s[rows[c], :]) * g_s[rows[c], :]).astype(o_ref.dtype)


def _rwkv(proj, act, B, S, mu_rkv, w0, a0, k_k, k_a, r_k, lnx_w, lnx_b, w_up_p, a_up_p, g_up, side,
          zero_shape):
    T = B * S
    tr = RWKV_TR
    nt = S // tr
    sub = tr // BF16_SUBLANES
    nsub = S // BF16_SUBLANES
    npair = RWKV_WIDTH // PAIR
    rcol, kcol, vcol = COL_R // PAIR, COL_KR // PAIR, COL_VR // PAIR
    side_rows = side.shape[0] // (B * npair * nt)
    zero_rows = zero_shape[0] // (B * npair * nt)
    step = lambda b, h, t: ((b * npair + h) * nt + t, 0)

    def cur(col):
        return pl.BlockSpec((tr, PAIR), lambda b, h, t: (b * nt + t, col + h))

    def prev(col):
        return pl.BlockSpec((BF16_SUBLANES, PAIR),
                            lambda b, h, t: (b * nsub + jnp.maximum(t * sub - 1, 0), col + h))

    def vec(off):
        return pl.BlockSpec((1, PAIR), lambda b, h, t: (0, off + h))

    in_specs = [
        cur(rcol), cur(kcol), cur(vcol),
        pl.BlockSpec((tr, LORA_PAD), lambda b, h, t: (b * nt + t, 0)),
        prev(rcol), prev(kcol), prev(vcol),
        vec(0), vec(npair), vec(2 * npair),
        vec(0), vec(0), vec(0), vec(0), vec(0), vec(0), vec(0),
        pl.BlockSpec((LANES, PAIR), lambda b, h, t: (0, h)),
        pl.BlockSpec((LANES, PAIR), lambda b, h, t: (0, h)),
        pl.BlockSpec((GATE_LORA, PAIR), lambda b, h, t: (0, h)),
        pl.BlockSpec((side_rows, side.shape[1]), step),
    ]
    scratch = [pltpu.VMEM((PAIR, PAIR), f32)] + [pltpu.VMEM((tr, PAIR), f32)] * 8
    return pl.pallas_call(
        _rwkv_kernel,
        out_shape=(jax.ShapeDtypeStruct((T, RWKV_WIDTH), bf16),
                   jax.ShapeDtypeStruct(side.shape, bf16),
                   jax.ShapeDtypeStruct(zero_shape, f32)),
        grid=(B, npair, nt),
        in_specs=in_specs,
        out_specs=(pl.BlockSpec((tr, PAIR), lambda b, h, t: (b * nt + t, h)),
                   pl.BlockSpec((side_rows, side.shape[1]), step),
                   pl.BlockSpec((zero_rows, zero_shape[1]), step)),
        scratch_shapes=scratch,
        compiler_params=pltpu.CompilerParams(
            dimension_semantics=("parallel", "parallel", "arbitrary"), vmem_limit_bytes=VMEM_LIMIT),
        name="rwkv7",
    )(proj, proj, proj, act, proj, proj, proj,
      mu_rkv, mu_rkv, mu_rkv, w0, a0, k_k, k_a, r_k, lnx_w, lnx_b, w_up_p, a_up_p, g_up, side)


def _layer_norm(z, w, b):
    mu = jnp.mean(z, axis=-1, keepdims=True)
    d = z - mu
    var = jnp.mean(d * d, axis=-1, keepdims=True)
    return d * lax.rsqrt(var + LN_EPS) * w + b


def _merge_kernel(x_ref, ya_ref, yb_ref, ga_ref, gb_ref, pa_ref, pb_ref, wo_ref, l1w_ref, l1b_ref,
                  wrt_ref, brt_ref,
                  h_ref, topi_ref, gate_ref, rank_ref, cnt_ref,
                  run_ref):
    i = pl.program_id(0)
    tm = x_ref.shape[0]
    ne = wrt_ref.shape[0]

    @pl.when(i == 0)
    def _():
        run_ref[...] = jnp.zeros_like(run_ref)

    ma = _dot(ya_ref[...], pa_ref[...])
    mb = _dot(yb_ref[...], pb_ref[...])
    merged = (jax.nn.sigmoid(ga_ref[...].astype(f32)) * ma
              + jax.nn.sigmoid(gb_ref[...].astype(f32)) * mb)
    mix = _dot(merged.astype(bf16), wo_ref[...])
    h = _layer_norm(DEEPNORM_ALPHA * x_ref[...] + mix, l1w_ref[...], l1b_ref[...])
    h_ref[...] = h

    h_hi, h_lo = _split_bf16(h)
    w_hi, w_lo = _split_bf16(wrt_ref[...])
    prod = _dot_nt(jnp.concatenate([w_hi, w_lo], axis=0), jnp.concatenate([h_hi, h_lo], axis=0))
    lg = prod[0:ne, 0:tm] + (prod[0:ne, tm:] + prod[ne:, 0:tm]) + brt_ref[...]
    eidx = lax.broadcasted_iota(i32, (ne, tm), 0)
    slot = lax.broadcasted_iota(i32, (F32_SUBLANES, tm), 0)
    vals, idxs = [], []
    for _ in range(TOP_K):
        m = jnp.max(lg, axis=0, keepdims=True)
        idx = jnp.min(jnp.where(lg == m, eidx, ne), axis=0, keepdims=True)
        vals.append(m)
        idxs.append(idx)
        lg = jnp.where(eidx == idx, -jnp.inf, lg)
    exps = [jnp.exp(vk - vals[0]) for vk in vals]
    denom = exps[0] + exps[1] + exps[2] + exps[3]

    onehots = [jnp.where(eidx == idx, 1.0, 0.0).astype(f32) for idx in idxs]
    oh = onehots[0] + onehots[1] + onehots[2] + onehots[3]
    tri = jnp.where(lax.broadcasted_iota(i32, (tm, tm), 0) < lax.broadcasted_iota(i32, (tm, tm), 1),
                    1.0, 0.0).astype(bf16)
    before = run_ref[...] + _dot(oh.astype(bf16), tri)

    topi_o = jnp.zeros((F32_SUBLANES, tm), i32)
    gate_o = jnp.zeros((F32_SUBLANES, tm), f32)
    rank_o = jnp.zeros((F32_SUBLANES, tm), i32)
    for kslot in range(TOP_K):
        rk = jnp.sum(onehots[kslot] * before, axis=0, keepdims=True).astype(i32)
        topi_o = jnp.where(slot == kslot, idxs[kslot], topi_o)
        gate_o = jnp.where(slot == kslot, exps[kslot] / denom, gate_o)
        rank_o = jnp.where(slot == kslot, rk, rank_o)
    topi_ref[...] = topi_o
    gate_ref[...] = gate_o
    rank_ref[...] = rank_o
    run = run_ref[...] + jnp.sum(oh, axis=1, keepdims=True)
    run_ref[...] = run
    cnt_ref[...] = run


def _merge_router(x2, ya, yb, proj, pa, pb, wo, l1w, l1b, wrt, brt):
    T, D = x2.shape
    tm = MERGE_TM
    ne = wrt.shape[0]
    gcol = COL_GATE // D

    def const(shape):
        return pl.BlockSpec(shape, lambda i: (0, 0), pipeline_mode=pl.Buffered(1))

    out_shape = (jax.ShapeDtypeStruct((T, D), f32),
                 jax.ShapeDtypeStruct((F32_SUBLANES, T), i32),
                 jax.ShapeDtypeStruct((F32_SUBLANES, T), f32),
                 jax.ShapeDtypeStruct((F32_SUBLANES, T), i32),
                 jax.ShapeDtypeStruct((ne, 1), f32))
    return pl.pallas_call(
        _merge_kernel,
        out_shape=out_shape,
        grid=(T // tm,),
        in_specs=[pl.BlockSpec((tm, D), lambda i: (i, 0)),
                  pl.BlockSpec((tm, ATT_WIDTH), lambda i: (i, 0)),
                  pl.BlockSpec((tm, RWKV_WIDTH), lambda i: (i, 0)),
                  pl.BlockSpec((tm, D), lambda i: (i, gcol)),
                  pl.BlockSpec((tm, D), lambda i: (i, gcol + 1)),
                  const(pa.shape), const(pb.shape), const(wo.shape),
                  const((1, D)), const((1, D)), const(wrt.shape), const((ne, 1))],
        out_specs=(pl.BlockSpec((tm, D), lambda i: (i, 0)),
                   pl.BlockSpec((F32_SUBLANES, tm), lambda i: (0, i)),
                   pl.BlockSpec((F32_SUBLANES, tm), lambda i: (0, i)),
                   pl.BlockSpec((F32_SUBLANES, tm), lambda i: (0, i)),
                   pl.BlockSpec((ne, 1), lambda i: (0, 0))),
        scratch_shapes=[pltpu.VMEM((ne, 1), f32)],
        compiler_params=pltpu.CompilerParams(
            dimension_semantics=("arbitrary",), vmem_limit_bytes=VMEM_LIMIT),
        name="merge_ln1_router",
    )(x2, ya, yb, proj, proj, pa, pb, wo, l1w, l1b, wrt, brt)


def _dispatch_kernel(dest_ref, h_ref, xs_in_hbm, xs_hbm, sem):
    del xs_in_hbm
    tm = h_ref.shape[0]

    def row_copy(t, k):
        return pltpu.make_async_copy(h_ref.at[pl.ds(t, 1)],
                                     xs_hbm.at[pl.ds(dest_ref[t * TOP_K + k], 1)], sem)

    def start(t, c):
        for k in range(TOP_K):
            row_copy(t, k).start()
        return c

    def wait(t, c):
        for k in range(TOP_K):
            row_copy(t, k).wait()
        return c

    lax.fori_loop(0, tm, start, 0, unroll=DMA_UNROLL)
    lax.fori_loop(0, tm, wait, 0, unroll=DMA_UNROLL)


def _dispatch(dest_flat, h, xs0):
    T, D = h.shape
    n = COMBINE_TM * TOP_K
    return pl.pallas_call(
        _dispatch_kernel,
        out_shape=jax.ShapeDtypeStruct(xs0.shape, xs0.dtype),
        grid=(T * TOP_K // n,),
        in_specs=[pl.BlockSpec((n,), lambda i: (i,), memory_space=pltpu.SMEM),
                  pl.BlockSpec((n // TOP_K, D), lambda i: (i, 0)),
                  pl.BlockSpec(memory_space=pl.ANY)],
        out_specs=pl.BlockSpec(memory_space=pl.ANY),
        scratch_shapes=[pltpu.SemaphoreType.DMA(())],
        input_output_aliases={2: 0},
        compiler_params=pltpu.CompilerParams(
            dimension_semantics=("arbitrary",), has_side_effects=True, vmem_limit_bytes=VMEM_LIMIT),
        name="moe_dispatch",
    )(dest_flat, h, xs0)


def _expert_kernel(be_ref, nu_ref, nv_ref, x_ref, wg_ref, wu_ref, bg_ref, bu_ref, wd_ref, bd_ref, o_ref, xb_ref):
    del be_ref
    i = pl.program_id(0)
    j = pl.program_id(1)
    used = i < nu_ref[0]
    half = o_ref.shape[0] // 2
    top_only = nv_ref[i] <= half

    @pl.when(jnp.logical_and(used, j == 0))
    def _():
        xb_ref[...] = x_ref[...].astype(bf16)

    @pl.when(jnp.logical_and(jnp.logical_not(used), j == 0))
    def _():
        o_ref[...] = jnp.zeros_like(o_ref)

    def ffn(rows, first):
        xb = xb_ref[rows, :]
        g = _dot(xb, wg_ref[0]) + bg_ref[0]
        u = _dot(xb, wu_ref[0]) + bu_ref[0]
        g = jnp.minimum(g, SWIGLU_LIMIT)
        u = jnp.clip(u, -SWIGLU_LIMIT, SWIGLU_LIMIT)
        act = (u + 1.0) * (g * jax.nn.sigmoid(SWIGLU_ALPHA * g))
        y = _dot(act.astype(bf16), wd_ref[0])
        if first:
            o_ref[rows, :] = y + bd_ref[0]
        else:
            o_ref[rows, :] += y

    for first in (True, False):
        at_step = (j == 0) if first else (j > 0)

        @pl.when(jnp.logical_and(jnp.logical_and(used, at_step), jnp.logical_not(top_only)))
        def _():
            ffn(slice(None), first)

        @pl.when(jnp.logical_and(jnp.logical_and(used, at_step), top_only))
        def _():
            ffn(slice(0, half), first)
            if first:
                o_ref[half:, :] = jnp.zeros((half, o_ref.shape[1]), o_ref.dtype)


def _experts(blk_e, nused, nvalid, xs, w_gu, b_gu, w_down, b_down):
    R, D = xs.shape
    E, _, F2 = w_gu.shape
    F = F2 // 2
    rb, tf = MOE_RB, min(MOE_TF, F)
    nf = F // tf

    def jj(i, j, nu):
        return jnp.where(i < nu[0], j, 0)

    grid_spec = pltpu.PrefetchScalarGridSpec(
        num_scalar_prefetch=3,
        grid=(R // rb, nf),
        in_specs=[pl.BlockSpec((rb, D), lambda i, j, be, nu, nv: (jnp.where(i < nu[0], i, 0), 0)),
                  pl.BlockSpec((1, D, tf), lambda i, j, be, nu, nv: (be[i], 0, jj(i, j, nu))),
                  pl.BlockSpec((1, D, tf), lambda i, j, be, nu, nv: (be[i], 0, nf + jj(i, j, nu))),
                  pl.BlockSpec((1, 1, tf), lambda i, j, be, nu, nv: (be[i], 0, jj(i, j, nu))),
                  pl.BlockSpec((1, 1, tf), lambda i, j, be, nu, nv: (be[i], 0, nf + jj(i, j, nu))),
                  pl.BlockSpec((1, tf, D), lambda i, j, be, nu, nv: (be[i], jj(i, j, nu), 0)),
                  pl.BlockSpec((1, 1, D), lambda i, j, be, nu, nv: (be[i], 0, 0))],
        out_specs=pl.BlockSpec((rb, D), lambda i, j, be, nu, nv: (i, 0)),
        scratch_shapes=[pltpu.VMEM((rb, D), bf16)],
    )
    return pl.pallas_call(
        _expert_kernel,
        out_shape=jax.ShapeDtypeStruct((R, D), f32),
        grid_spec=grid_spec,
        compiler_params=pltpu.CompilerParams(
            dimension_semantics=("parallel", "arbitrary"), vmem_limit_bytes=VMEM_LIMIT),
        name="moe_experts",
    )(blk_e, nused, nvalid, xs, w_gu, w_gu, b_gu, b_gu, w_down, b_down)


def _combine_kernel(dest_ref, dest_next_ref, h_ref, gate_ref, l2w_ref, l2b_ref, y_hbm, o_ref, buf, sem):
    i = pl.program_id(0)
    tm = h_ref.shape[0]
    slot = i % 2

    def row_copy(d_ref, sl, t, k):
        return pltpu.make_async_copy(y_hbm.at[pl.ds(d_ref[t * TOP_K + k], 1)],
                                     buf.at[sl, k, pl.ds(t, 1)], sem.at[sl])

    def start_tile(d_ref, sl):
        def body(t, c):
            for k in range(TOP_K):
                row_copy(d_ref, sl, t, k).start()
            return c
        lax.fori_loop(0, tm, body, 0, unroll=DMA_UNROLL)

    def wait_tile(d_ref, sl):
        def body(t, c):
            for k in range(TOP_K):
                row_copy(d_ref, sl, t, k).wait()
            return c
        lax.fori_loop(0, tm, body, 0, unroll=DMA_UNROLL)

    @pl.when(i == 0)
    def _():
        start_tile(dest_ref, 0)

    @pl.when(i + 1 < pl.num_programs(0))
    def _():
        start_tile(dest_next_ref, 1 - slot)

    wait_tile(dest_ref, slot)

    gate = gate_ref[...]
    ffn = gate[:, 0:1] * buf[slot, 0]
    for kslot in range(1, TOP_K):
        ffn = ffn + gate[:, kslot:kslot + 1] * buf[slot, kslot]
    o_ref[...] = _layer_norm(DEEPNORM_ALPHA * h_ref[...] + ffn, l2w_ref[...], l2b_ref[...])


def _combine(dest_flat, h, gate, l2w, l2b, y):
    T, D = h.shape
    tm = COMBINE_TM
    n = tm * TOP_K
    steps = T // tm
    return pl.pallas_call(
        _combine_kernel,
        out_shape=jax.ShapeDtypeStruct((T, D), f32),
        grid=(steps,),
        in_specs=[pl.BlockSpec((n,), lambda i: (i,), memory_space=pltpu.SMEM),
                  pl.BlockSpec((n,), lambda i: (jnp.minimum(i + 1, steps - 1),), memory_space=pltpu.SMEM),
                  pl.BlockSpec((tm, D), lambda i: (i, 0)),
                  pl.BlockSpec((tm, LANES), lambda i: (i, 0)),
                  pl.BlockSpec((1, D), lambda i: (0, 0)),
                  pl.BlockSpec((1, D), lambda i: (0, 0)),
                  pl.BlockSpec(memory_space=pl.ANY)],
        out_specs=pl.BlockSpec((tm, D), lambda i: (i, 0)),
        scratch_shapes=[pltpu.VMEM((2, TOP_K, tm, D), f32), pltpu.SemaphoreType.DMA((2,))],
        compiler_params=pltpu.CompilerParams(
            dimension_semantics=("arbitrary",), vmem_limit_bytes=VMEM_LIMIT),
        name="moe_combine_ln2",
    )(dest_flat, dest_flat, h, gate, l2w, l2b, y)


def _pack_w_in(w_in):
    o1 = 3 * ATT_WIDTH
    o2 = o1 + 3 * RWKV_WIDTH
    o3 = o2 + DECAY_LORA
    o4 = o3 + ICLR_LORA
    o5 = o4 + GATE_LORA
    padc = lambda w, n: jnp.pad(w, ((0, 0), (0, n - w.shape[1])))
    parts = [w_in[:, o5:], w_in[:, :o2], padc(w_in[:, o2:o3], LANES), padc(w_in[:, o3:o4], LANES),
             w_in[:, o4:o5]]
    return jnp.concatenate(parts, axis=1).astype(bf16)


def _token_mixing(x2, B, S, w_in, rel_bias, shift_mu, w0, w_up, a0, a_up, g_up, k_k, k_a, r_k,
                  lnx_w, lnx_b, side_a, side_b, zero_shape):
    T, D = x2.shape
    wp = _pack_w_in(w_in)
    proj = _in_proj(x2, wp, min(1024, T), 1536)

    y_a, side_a = _attention(proj, _attn_bias_table(rel_bias), B, S, side_a)

    W = RWKV_WIDTH
    row = lambda p: p.reshape(1, -1).astype(f32)
    padv = lambda p, n: jnp.pad(p, (0, n - p.shape[0]))
    mu_rkv = row(shift_mu[:3 * W])
    mu_lora = row(jnp.concatenate([padv(shift_mu[3 * W:3 * W + DECAY_LORA], LANES),
                                   padv(shift_mu[3 * W + DECAY_LORA:3 * W + DECAY_LORA + ICLR_LORA], LANES),
                                   shift_mu[3 * W + DECAY_LORA + ICLR_LORA:]]))
    padr = lambda w: jnp.pad(w, ((0, LANES - w.shape[0]), (0, 0))).astype(bf16)
    act = _lora_prep(proj, B, S, mu_lora)
    y_b, side_b, zeros = _rwkv(proj, act, B, S, mu_rkv, row(w0), row(a0), row(k_k), row(k_a), row(r_k),
                               row(lnx_w), row(lnx_b), padr(w_up), padr(a_up), g_up.astype(bf16), side_b,
                               zero_shape)
    return proj, y_a, y_b, side_a, side_b, zeros


def _moe_routing(topi, rank, counts, n_experts, rb, n_blocks):
    M = topi.shape[0] * TOP_K
    counts = counts.astype(i32)
    padded = (counts + rb - 1) // rb * rb
    pad_ends = jnp.cumsum(padded)
    pad_starts = pad_ends - padded
    onehot = topi[..., None] == jnp.arange(n_experts, dtype=i32)
    dest = (jnp.sum(jnp.where(onehot, pad_starts, 0), axis=-1) + rank).reshape(M)
    blk_start = jnp.arange(n_blocks, dtype=i32)[:, None] * rb
    blk_e = jnp.minimum(jnp.sum((pad_ends[None, :] <= blk_start).astype(i32), axis=-1), n_experts - 1)
    nused = (pad_ends[-1:] // rb).astype(i32)
    nvalid = jnp.clip((pad_starts + counts)[blk_e] - blk_start[:, 0], 0, rb).astype(i32)
    return dest.astype(i32), blk_e, nused, nvalid


def kernel(x, w_in, rel_bias, shift_mu, w0, w_up, a0, a_up, g_up, k_k, k_a, r_k, lnx_w, lnx_b,
           proj_a, proj_b, w_out, ln1_w, ln1_b, w_router, b_router, w_gu, b_gu, w_down, b_down,
           ln2_w, ln2_b):
    B, S, D = x.shape
    T = B * S
    h = x.reshape(T, D)
    for l in range(DEPTH):
        E, F = w_down.shape[1], w_down.shape[2]
        n_blocks = (T * TOP_K + E * (MOE_RB - 1) + MOE_RB - 1) // MOE_RB
        proj, y_a, y_b, w_down_b, w_gu_b, xs0 = _token_mixing(
            h, B, S, w_in[l], rel_bias[l], shift_mu[l], w0[l], w_up[l], a0[l], a_up[l], g_up[l],
            k_k[l], k_a[l], r_k[l], lnx_w[l], lnx_b[l],
            w_down[l].reshape(E * F, D), w_gu[l].reshape(E * D, 2 * F), (n_blocks * MOE_RB, D))
        h1, topi, gate, rank, counts = _merge_router(
            h, y_a, y_b, proj, proj_a[l].astype(bf16), proj_b[l].astype(bf16), w_out[l].astype(bf16),
            ln1_w[l].reshape(1, D), ln1_b[l].reshape(1, D), w_router[l].T, b_router[l].reshape(E, 1))
        gate = jnp.pad(gate[:TOP_K].T, ((0, 0), (0, LANES - TOP_K)))
        dest, blk_e, nused, nvalid = _moe_routing(
            topi[:TOP_K].T, rank[:TOP_K].T, counts[:, 0], E, MOE_RB, n_blocks)
        xs = _dispatch(dest, h1, xs0)
        y = _experts(blk_e, nused, nvalid, xs, w_gu_b.reshape(E, D, 2 * F), b_gu[l].reshape(E, 1, -1),
                     w_down_b.reshape(E, F, D), b_down[l].reshape(E, 1, D))
        h = _combine(dest, h1, gate, ln2_w[l].reshape(1, D), ln2_b[l].reshape(1, D), y)
    return h.reshape(B, S, D)
```

```python
import jax
import jax.numpy as jnp
from jax import lax
from jax.experimental import pallas as pl
from jax.experimental.pallas import tpu as pltpu

f32 = jnp.float32
bf16 = jnp.bfloat16
i32 = jnp.int32

CHUNK = 64
LEFT_CHUNKS = 8
HEAD_DIM = 64
ATT_WIDTH = 1024
MAX_PAST_DIST = 256
REL_TABLE = MAX_PAST_DIST + CHUNK
RWKV_WIDTH = 1024
DECAY_LORA = 96
ICLR_LORA = 96
GATE_LORA = 256
TOP_K = 4
SWIGLU_LIMIT = 7.0
SWIGLU_ALPHA = 1.702
LN_EPS = 1e-5
GN_EPS = 64e-5
DEPTH = 1
DEEPNORM_ALPHA = (2 * DEPTH) ** 0.25

LANES = 128
PAIR = 2 * HEAD_DIM
F32_SUBLANES = 8
BF16_SUBLANES = 16
VMEM_LIMIT = 56 * 1024 * 1024

NEG = -1e30

LORA_PAD = 512
COL_GATE = 0
COL_Q = 4096
COL_K = COL_Q + ATT_WIDTH
COL_V = COL_K + ATT_WIDTH
COL_R = COL_V + ATT_WIDTH
COL_KR = COL_R + RWKV_WIDTH
COL_VR = COL_KR + RWKV_WIDTH
COL_LORA = COL_VR + RWKV_WIDTH
PROJ_COLS = COL_LORA + LORA_PAD

ATT_TQ = 256
ATT_HB = 16
RWKV_TR = 1024
LORA_TR = 1024
MERGE_TM = 512
MOE_RB = 512
MOE_TF = 1024
COMBINE_TM = 256
DMA_UNROLL = 4


def _dot(a, b, precision=None):
    return jnp.dot(a, b, preferred_element_type=f32, precision=precision)


def _dot_nt(a, b):
    return lax.dot_general(a, b, (((1,), (1,)), ((), ())), preferred_element_type=f32)


def _dot_tn(a, b):
    return lax.dot_general(a, b, (((0,), (0,)), ((), ())), preferred_element_type=f32)


def _split_bf16(a):
    hi = a.astype(bf16)
    return hi, (a - hi.astype(f32)).astype(bf16)


def _dot_split_lhs(a, b_bf16):
    hi, lo = _split_bf16(a)
    return _dot(hi, b_bf16) + _dot(lo, b_bf16)


def _dot_split_rhs(a_bf16, b):
    hi, lo = _split_bf16(b)
    return _dot(a_bf16, hi) + _dot(a_bf16, lo)


def _inproj_kernel(x_ref, w_ref, o_ref, xb_ref):
    @pl.when(pl.program_id(1) == 0)
    def _():
        xb_ref[...] = x_ref[...].astype(bf16)

    o_ref[...] = _dot(xb_ref[...], w_ref[...]).astype(o_ref.dtype)


def _in_proj(x2, w, tm, tn):
    M, K = x2.shape
    N = w.shape[1]
    return pl.pallas_call(
        _inproj_kernel,
        out_shape=jax.ShapeDtypeStruct((M, N), bf16),
        grid=(M // tm, N // tn),
        in_specs=[pl.BlockSpec((tm, K), lambda i, j: (i, 0)),
                  pl.BlockSpec((K, tn), lambda i, j: (0, j))],
        out_specs=pl.BlockSpec((tm, tn), lambda i, j: (i, j)),
        scratch_shapes=[pltpu.VMEM((tm, K), bf16)],
        compiler_params=pltpu.CompilerParams(
            dimension_semantics=("parallel", "arbitrary"), vmem_limit_bytes=VMEM_LIMIT),
        name="in_proj",
    )(x2, w)


def _attn_bias_table(rel_bias):
    tq = ATT_TQ
    nk = 3 * tq
    H = rel_bias.shape[0]
    C = CHUNK
    nqc, nkc = tq // C, nk // C
    e = jnp.arange(-(nkc - 1), nqc)
    dist = (e[:, None] * C + jnp.arange(-(C - 1), C)[None, :]) + 2 * tq
    idx = jnp.clip(jnp.minimum(dist, MAX_PAST_DIST) + (C - 1), 0, REL_TABLE - 1)
    w = jnp.pad(rel_bias.astype(f32)[:, idx], ((0, 0), (0, 0), (0, 1)))
    ne = e.shape[0]
    G = jnp.broadcast_to(w[:, :, None, :], (H, ne, C, 2 * C)).reshape(H, ne, C * 2 * C)
    G = G[:, :, :C * (2 * C - 1)].reshape(H, ne, C, 2 * C - 1)
    blocks = jnp.swapaxes(G[:, :, :, C - 1:], 2, 3)
    qc = jnp.arange(nqc)[:, None]
    kc = jnp.arange(nkc)[None, :]
    d = qc + (2 * tq) // C - kc
    tiles = blocks[:, (qc - kc) + (nkc - 1)]
    tiles = jnp.where(((d >= 0) & (d <= LEFT_CHUNKS))[None, :, :, None, None], tiles, NEG)
    return jnp.transpose(tiles, (0, 1, 3, 2, 4)).reshape(H, tq, nk)


def _attn_kernel(q_ref, k0_ref, k1_ref, k2_ref, v0_ref, v1_ref, v2_ref, bias_ref, side_ref,
                 o_ref, side_o_ref):
    qt = pl.program_id(2)
    side_o_ref[...] = side_ref[...].astype(side_o_ref.dtype)
    tq = q_ref.shape[0]
    k_refs = (k0_ref, k1_ref, k2_ref)
    v_refs = (v0_ref, v1_ref, v2_ref)
    heads = range(ATT_HB)
    blocks = range(3)
    lane = lax.broadcasted_iota(i32, (1, PAIR), 1)
    head0 = lane < HEAD_DIM
    pair = [slice((h // 2) * PAIR, (h // 2 + 1) * PAIR) for h in heads]
    q = [jnp.where(head0 if h % 2 == 0 else jnp.logical_not(head0),
                   q_ref[:, pair[h]] * (HEAD_DIM ** -0.5), 0.0).astype(bf16) for h in heads]
    s = [[_dot_nt(q[h], k_refs[j][:, pair[h]]) + bias_ref[h, :, j * tq:(j + 1) * tq] for j in blocks]
         for h in heads]
    s = [[jnp.where(qt - 2 + j >= 0, s[h][j], NEG) if j < 2 else s[h][j] for j in blocks] for h in heads]
    m = [jnp.max(jnp.maximum(jnp.maximum(s[h][0], s[h][1]), s[h][2]), axis=-1, keepdims=True)
         for h in heads]
    p = [[jnp.exp((s[h][j] - m[h]).astype(bf16)) for j in blocks] for h in heads]
    ones = jnp.ones((tq, PAIR), bf16)
    v1 = [[jnp.concatenate([v_refs[j][:, pair[2 * g]], ones], axis=1) for j in blocks]
          for g in range(ATT_HB // 2)]
    acc = [_dot(p[h][0], v1[h // 2][0]) + _dot(p[h][1], v1[h // 2][1]) + _dot(p[h][2], v1[h // 2][2])
           for h in heads]
    out = [acc[h][:, 0:PAIR] / acc[h][:, PAIR:PAIR + 1] for h in heads]
    for g in range(ATT_HB // 2):
        o_ref[:, pair[2 * g]] = jnp.where(head0, out[2 * g], out[2 * g + 1]).astype(o_ref.dtype)


def _attention(proj, bias, B, S, side):
    T = B * S
    tq = ATT_TQ
    nt = S // tq
    width = ATT_HB * HEAD_DIM
    ngrp = ATT_WIDTH // width
    side_spec = pl.BlockSpec((side.shape[0] // (B * nt * ngrp), side.shape[1]),
                             lambda h, b, t: ((h * B + b) * nt + t, 0))
    qcol, kcol, vcol = COL_Q // width, COL_K // width, COL_V // width

    def kv_spec(col, back):
        return pl.BlockSpec((tq, width),
                            lambda h, b, t: (b * nt + jnp.maximum(t - back, 0), col + h))

    return pl.pallas_call(
        _attn_kernel,
        out_shape=(jax.ShapeDtypeStruct((T, ATT_WIDTH), bf16), jax.ShapeDtypeStruct(side.shape, bf16)),
        grid=(ngrp, B, nt),
        in_specs=[pl.BlockSpec((tq, width), lambda h, b, t: (b * nt + t, qcol + h)),
                  kv_spec(kcol, 2), kv_spec(kcol, 1), kv_spec(kcol, 0),
                  kv_spec(vcol, 2), kv_spec(vcol, 1), kv_spec(vcol, 0),
                  pl.BlockSpec((ATT_HB, tq, 3 * tq), lambda h, b, t: (h, 0, 0)),
                  side_spec],
        out_specs=(pl.BlockSpec((tq, width), lambda h, b, t: (b * nt + t, h)), side_spec),
        compiler_params=pltpu.CompilerParams(
            dimension_semantics=("parallel", "parallel", "parallel"), vmem_limit_bytes=VMEM_LIMIT),
        name="chunk_attn",
    )(proj, proj, proj, proj, proj, proj, proj, bias, side)


def _lora_prep_kernel(lo_ref, lop_ref, mu_ref, o_ref):
    t = pl.program_id(1)
    tr = lo_ref.shape[0]
    p = lo_ref[...].astype(f32)
    prev_row = lop_ref[BF16_SUBLANES - 1:BF16_SUBLANES, :].astype(f32) * jnp.where(t == 0, 0.0, 1.0)
    row = lax.broadcasted_iota(i32, (tr, 1), 0)
    prev = jnp.where(row == 0, prev_row, pltpu.roll(p, 1, 0))
    lo = p + (prev - p) * mu_ref[...]
    o_ref[:, 0:LANES] = jnp.tanh(lo[:, 0:LANES]).astype(o_ref.dtype)
    o_ref[:, LANES:2 * LANES] = lo[:, LANES:2 * LANES].astype(o_ref.dtype)
    o_ref[:, 2 * LANES:] = jax.nn.sigmoid(lo[:, 2 * LANES:]).astype(o_ref.dtype)


def _lora_prep(proj, B, S, mu_lora):
    tr = LORA_TR
    nt = S // tr
    sub = tr // BF16_SUBLANES
    nsub = S // BF16_SUBLANES
    locol = COL_LORA // LORA_PAD
    return pl.pallas_call(
        _lora_prep_kernel,
        out_shape=jax.ShapeDtypeStruct((B * S, LORA_PAD), bf16),
        grid=(B, nt),
        in_specs=[pl.BlockSpec((tr, LORA_PAD), lambda b, t: (b * nt + t, locol)),
                  pl.BlockSpec((BF16_SUBLANES, LORA_PAD),
                               lambda b, t: (b * nsub + jnp.maximum(t * sub - 1, 0), locol)),
                  pl.BlockSpec((1, LORA_PAD), lambda b, t: (0, 0))],
        out_specs=pl.BlockSpec((tr, LORA_PAD), lambda b, t: (b * nt + t, 0)),
        compiler_params=pltpu.CompilerParams(
            dimension_semantics=("parallel", "parallel"), vmem_limit_bytes=VMEM_LIMIT),
        name="rwkv7_lora_prep",
    )(proj, proj, mu_lora)


def _rwkv_kernel(r_ref, k_ref, v_ref, act_ref, rp_ref, kp_ref, vp_ref,
                 mur_ref, muk_ref, muv_ref, w0_ref, a0_ref, kkp_ref, kap_ref, rkp_ref,
                 lnw_ref, lnb_ref, wup_ref, aup_ref, gup_ref, side_ref,
                 o_ref, side_o_ref, zero_o_ref,
                 st_ref, r_s, lw_s, k_s, v_s, a_s, b_s, g_s, bon_s):
    t = pl.program_id(2)
    tr = r_ref.shape[0]
    C = CHUNK

    side_o_ref[...] = side_ref[...].astype(side_o_ref.dtype)
    zero_o_ref[...] = jnp.zeros_like(zero_o_ref)

    @pl.when(t == 0)
    def _():
        st_ref[...] = jnp.zeros_like(st_ref)

    not_first = jnp.where(t == 0, 0.0, 1.0).astype(f32)
    row = lax.broadcasted_iota(i32, (tr, 1), 0)

    def shift(p_ref, prev_ref, mu_ref):
        p = p_ref[...].astype(f32)
        prev_row = prev_ref[BF16_SUBLANES - 1:BF16_SUBLANES, :].astype(f32) * not_first
        prev = jnp.where(row == 0, prev_row, pltpu.roll(p, 1, 0))
        return p + (prev - p) * mu_ref[...]

    r = shift(r_ref, rp_ref, mur_ref)
    k = shift(k_ref, kp_ref, muk_ref)
    v = shift(v_ref, vp_ref, muv_ref)

    ri = lax.broadcasted_iota(i32, (PAIR, PAIR), 0)
    ci = lax.broadcasted_iota(i32, (PAIR, PAIR), 1)
    same_head = (ri // HEAD_DIM) == (ci // HEAD_DIM)
    head_ones = jnp.where(same_head, 1.0, 0.0).astype(bf16)
    head_avg = jnp.where(same_head, 1.0 / HEAD_DIM, 0.0).astype(bf16)

    z = w0_ref[...] + _dot(act_ref[:, 0:LANES], wup_ref[...])
    softplus_negz = jnp.maximum(-z, 0.0) + jnp.log(1.0 + jnp.exp(-jnp.abs(z)))
    lw = -jnp.exp(-softplus_negz - 0.5)
    alr = jax.nn.sigmoid(a0_ref[...] + _dot(act_ref[:, LANES:2 * LANES], aup_ref[...]))
    g = _dot(act_ref[:, 2 * LANES:], gup_ref[...])
    kk = k * kkp_ref[...]
    kk = kk / jnp.maximum(jnp.sqrt(_dot_split_lhs(kk * kk, head_ones)), 1e-12)
    k2 = k * (1.0 + (alr - 1.0) * kap_ref[...])
    bonus = _dot_split_lhs(r * k2 * rkp_ref[...], head_ones) * v

    r_s[...] = r
    lw_s[...] = lw
    k_s[...] = k2
    v_s[...] = v
    a_s[...] = -kk
    b_s[...] = kk * alr
    g_s[...] = g
    bon_s[...] = bonus

    lane = lax.broadcasted_iota(i32, (1, PAIR), 1)
    head0 = lane < HEAD_DIM
    tri_incl = jnp.where(lax.broadcasted_iota(i32, (C, C), 1) <= lax.broadcasted_iota(i32, (C, C), 0),
                         1.0, 0.0).astype(bf16)
    strict = ci < ri
    incl = ci <= ri
    eye = ci == ri
    lnw = lnw_ref[...]
    lnb = lnb_ref[...]

    def stack(x):
        return jnp.concatenate([jnp.where(head0, x, 0.0), jnp.where(head0, 0.0, x)], axis=0)

    nc = tr // C
    chunks = range(nc)
    rows = [slice(c * C, (c + 1) * C) for c in chunks]
    lwc = [lw_s[rw, :] for rw in rows]
    L = [_dot_split_rhs(tri_incl, x) for x in lwc]
    LC = [x[C - 1:C, :] for x in L]
    a_t, r_t, r_tb, bk_t, v_st = [], [], [], [], []
    for c in chunks:
        rw = rows[c]
        e_neg = jnp.exp(-L[c])
        a_t.append(stack(a_s[rw, :] * jnp.exp(L[c] - lwc[c])).astype(bf16))
        r_t.append(stack(r_s[rw, :] * jnp.exp(L[c])))
        r_tb.append(r_t[c].astype(bf16))
        bk_t.append(jnp.concatenate([stack(b_s[rw, :] * e_neg), stack(k_s[rw, :] * e_neg)], axis=0).astype(bf16))
        v_st.append(stack(v_s[rw, :]).astype(bf16))

    A = [_dot_nt(jnp.concatenate([a_t[c], r_tb[c]], axis=0), bk_t[c]) for c in chunks]
    a_ab = [jnp.where(strict, x[0:PAIR, 0:PAIR], 0.0) for x in A]
    a_kk = [jnp.concatenate([jnp.where(strict, x[0:PAIR, PAIR:], 0.0),
                             jnp.where(incl, x[PAIR:, PAIR:], 0.0)], axis=0).astype(bf16) for x in A]
    a_rb = [jnp.where(incl, x[PAIR:, 0:PAIR], 0.0).astype(bf16) for x in A]

    p = a_ab
    tinv = [jnp.where(eye, 1.0, 0.0).astype(f32) + x for x in a_ab]
    for _ in range(5):
        pb = [x.astype(bf16) for x in p]
        p = [_dot(x, x) for x in pb]
        tinv = [tv + _dot(tv.astype(bf16), x.astype(bf16)) for tv, x in zip(tinv, p)]

    av = [_dot(a_kk[c], v_st[c]) for c in chunks]
    wu = [_dot(tinv[c].astype(bf16), jnp.concatenate([a_t[c], av[c][0:PAIR].astype(bf16)], axis=1))
          for c in chunks]
    wub = [x.astype(bf16) for x in wu]
    ry = [jnp.concatenate([r_t[c], av[c][PAIR:]], axis=1) + _dot(a_rb[c], wub[c]) for c in chunks]
    ry = [x[0:C] + x[C:] for x in ry]

    mct, gct = [], []
    for c in chunks:
        rw = rows[c]
        e_end = jnp.exp(LC[c] - L[c])
        b_h = stack(b_s[rw, :] * e_end).astype(bf16)
        k_h = stack(k_s[rw, :] * e_end).astype(bf16)
        mct.append(jnp.where(eye, jnp.exp(LC[c]), 0.0) + _dot_tn(b_h, wub[c][:, 0:PAIR]))
        gct.append(_dot_tn(jnp.concatenate([b_h, k_h], axis=0),
                           jnp.concatenate([wub[c][:, PAIR:], v_st[c]], axis=0)))

    st = st_ref[...]
    y = []
    for c in chunks:
        stb = st.astype(bf16)
        y.append(_dot(ry[c][:, 0:PAIR].astype(bf16), stb) + ry[c][:, PAIR:])
        st = _dot(mct[c].astype(bf16), stb) + gct[c]
    st_ref[...] = st

    mu = [_dot(x.astype(bf16), head_avg) for x in y]
    d = [x - m for x, m in zip(y, mu)]
    var = [_dot((x * x).astype(bf16), head_avg) for x in d]
    for c in chunks:
        yn = d[c] * lax.rsqrt(var[c] + GN_EPS) * lnw + lnb
        o_ref[rows[c], :] = ((yn + bon_s[rows[c], :]) * g_s[rows[c], :]).astype(o_ref.dtype)


def _rwkv(proj, act, B, S, mu_rkv, w0, a0, k_k, k_a, r_k, lnx_w, lnx_b, w_up_p, a_up_p, g_up, side,
          zero_shape):
    T = B * S
    tr = RWKV_TR
    nt = S // tr
    sub = tr // BF16_SUBLANES
    nsub = S // BF16_SUBLANES
    npair = RWKV_WIDTH // PAIR
    rcol, kcol, vcol = COL_R // PAIR, COL_KR // PAIR, COL_VR // PAIR
    side_rows = side.shape[0] // (B * npair * nt)
    zero_rows = zero_shape[0] // (B * npair * nt)
    step = lambda b, h, t: ((b * npair + h) * nt + t, 0)

    def cur(col):
        return pl.BlockSpec((tr, PAIR), lambda b, h, t: (b * nt + t, col + h))

    def prev(col):
        return pl.BlockSpec((BF16_SUBLANES, PAIR),
                            lambda b, h, t: (b * nsub + jnp.maximum(t * sub - 1, 0), col + h))

    def vec(off):
        return pl.BlockSpec((1, PAIR), lambda b, h, t: (0, off + h))

    in_specs = [
        cur(rcol), cur(kcol), cur(vcol),
        pl.BlockSpec((tr, LORA_PAD), lambda b, h, t: (b * nt + t, 0)),
        prev(rcol), prev(kcol), prev(vcol),
        vec(0), vec(npair), vec(2 * npair),
        vec(0), vec(0), vec(0), vec(0), vec(0), vec(0), vec(0),
        pl.BlockSpec((LANES, PAIR), lambda b, h, t: (0, h)),
        pl.BlockSpec((LANES, PAIR), lambda b, h, t: (0, h)),
        pl.BlockSpec((GATE_LORA, PAIR), lambda b, h, t: (0, h)),
        pl.BlockSpec((side_rows, side.shape[1]), step),
    ]
    scratch = [pltpu.VMEM((PAIR, PAIR), f32)] + [pltpu.VMEM((tr, PAIR), f32)] * 8
    return pl.pallas_call(
        _rwkv_kernel,
        out_shape=(jax.ShapeDtypeStruct((T, RWKV_WIDTH), bf16),
                   jax.ShapeDtypeStruct(side.shape, bf16),
                   jax.ShapeDtypeStruct(zero_shape, f32)),
        grid=(B, npair, nt),
        in_specs=in_specs,
        out_specs=(pl.BlockSpec((tr, PAIR), lambda b, h, t: (b * nt + t, h)),
                   pl.BlockSpec((side_rows, side.shape[1]), step),
                   pl.BlockSpec((zero_rows, zero_shape[1]), step)),
        scratch_shapes=scratch,
        compiler_params=pltpu.CompilerParams(
            dimension_semantics=("parallel", "parallel", "arbitrary"), vmem_limit_bytes=VMEM_LIMIT),
        name="rwkv7",
    )(proj, proj, proj, act, proj, proj, proj,
      mu_rkv, mu_rkv, mu_rkv, w0, a0, k_k, k_a, r_k, lnx_w, lnx_b, w_up_p, a_up_p, g_up, side)


def _layer_norm(z, w, b):
    mu = jnp.mean(z, axis=-1, keepdims=True)
    d = z - mu
    var = jnp.mean(d * d, axis=-1, keepdims=True)
    return d * lax.rsqrt(var + LN_EPS) * w + b


def _merge_kernel(x_ref, ya_ref, yb_ref, ga_ref, gb_ref, pa_ref, pb_ref, wo_ref, l1w_ref, l1b_ref,
                  wrt_ref, brt_ref,
                  h_ref, topi_ref, gate_ref, rank_ref, cnt_ref,
                  run_ref):
    i = pl.program_id(0)
    tm = x_ref.shape[0]
    ne = wrt_ref.shape[0]

    @pl.when(i == 0)
    def _():
        run_ref[...] = jnp.zeros_like(run_ref)

    halves = [slice(0, tm // 2), slice(tm // 2, tm)]
    ma = [_dot(ya_ref[rw, :], pa_ref[...]) for rw in halves]
    mb = [_dot(yb_ref[rw, :], pb_ref[...]) for rw in halves]
    merged = [(jax.nn.sigmoid(ga_ref[rw, :].astype(f32)) * a
               + jax.nn.sigmoid(gb_ref[rw, :].astype(f32)) * b).astype(bf16)
              for rw, a, b in zip(halves, ma, mb)]
    mix = [_dot(m, wo_ref[...]) for m in merged]
    h = jnp.concatenate([_layer_norm(DEEPNORM_ALPHA * x_ref[rw, :] + mx, l1w_ref[...], l1b_ref[...])
                         for rw, mx in zip(halves, mix)], axis=0)
    h_ref[...] = h

    h_hi, h_lo = _split_bf16(h)
    w_hi, w_lo = _split_bf16(wrt_ref[...])
    prod = _dot_nt(jnp.concatenate([w_hi, w_lo], axis=0), jnp.concatenate([h_hi, h_lo], axis=0))
    lg = prod[0:ne, 0:tm] + (prod[0:ne, tm:] + prod[ne:, 0:tm]) + brt_ref[...]
    eidx = lax.broadcasted_iota(i32, (ne, tm), 0)
    slot = lax.broadcasted_iota(i32, (F32_SUBLANES, tm), 0)
    vals, idxs = [], []
    for _ in range(TOP_K):
        m = jnp.max(lg, axis=0, keepdims=True)
        idx = jnp.min(jnp.where(lg == m, eidx, ne), axis=0, keepdims=True)
        vals.append(m)
        idxs.append(idx)
        lg = jnp.where(eidx == idx, -jnp.inf, lg)
    exps = [jnp.exp(vk - vals[0]) for vk in vals]
    denom = exps[0] + exps[1] + exps[2] + exps[3]

    onehots = [jnp.where(eidx == idx, 1.0, 0.0).astype(f32) for idx in idxs]
    oh = onehots[0] + onehots[1] + onehots[2] + onehots[3]
    tri = jnp.where(lax.broadcasted_iota(i32, (tm, tm), 0) < lax.broadcasted_iota(i32, (tm, tm), 1),
                    1.0, 0.0).astype(bf16)
    before = run_ref[...] + _dot(oh.astype(bf16), tri)

    topi_o = jnp.zeros((F32_SUBLANES, tm), i32)
    gate_o = jnp.zeros((F32_SUBLANES, tm), f32)
    rank_o = jnp.zeros((F32_SUBLANES, tm), i32)
    for kslot in range(TOP_K):
        rk = jnp.sum(onehots[kslot] * before, axis=0, keepdims=True).astype(i32)
        topi_o = jnp.where(slot == kslot, idxs[kslot], topi_o)
        gate_o = jnp.where(slot == kslot, exps[kslot] / denom, gate_o)
        rank_o = jnp.where(slot == kslot, rk, rank_o)
    topi_ref[...] = topi_o
    gate_ref[...] = gate_o
    rank_ref[...] = rank_o
    run = run_ref[...] + jnp.sum(oh, axis=1, keepdims=True)
    run_ref[...] = run
    cnt_ref[...] = run


def _merge_router(x2, ya, yb, proj, pa, pb, wo, l1w, l1b, wrt, brt):
    T, D = x2.shape
    tm = MERGE_TM
    ne = wrt.shape[0]
    gcol = COL_GATE // D

    def const(shape):
        return pl.BlockSpec(shape, lambda i: (0, 0), pipeline_mode=pl.Buffered(1))

    out_shape = (jax.ShapeDtypeStruct((T, D), f32),
                 jax.ShapeDtypeStruct((F32_SUBLANES, T), i32),
                 jax.ShapeDtypeStruct((F32_SUBLANES, T), f32),
                 jax.ShapeDtypeStruct((F32_SUBLANES, T), i32),
                 jax.ShapeDtypeStruct((ne, 1), f32))
    return pl.pallas_call(
        _merge_kernel,
        out_shape=out_shape,
        grid=(T // tm,),
        in_specs=[pl.BlockSpec((tm, D), lambda i: (i, 0)),
                  pl.BlockSpec((tm, ATT_WIDTH), lambda i: (i, 0)),
                  pl.BlockSpec((tm, RWKV_WIDTH), lambda i: (i, 0)),
                  pl.BlockSpec((tm, D), lambda i: (i, gcol)),
                  pl.BlockSpec((tm, D), lambda i: (i, gcol + 1)),
                  const(pa.shape), const(pb.shape), const(wo.shape),
                  const((1, D)), const((1, D)), const(wrt.shape), const((ne, 1))],
        out_specs=(pl.BlockSpec((tm, D), lambda i: (i, 0)),
                   pl.BlockSpec((F32_SUBLANES, tm), lambda i: (0, i)),
                   pl.BlockSpec((F32_SUBLANES, tm), lambda i: (0, i)),
                   pl.BlockSpec((F32_SUBLANES, tm), lambda i: (0, i)),
                   pl.BlockSpec((ne, 1), lambda i: (0, 0))),
        scratch_shapes=[pltpu.VMEM((ne, 1), f32)],
        compiler_params=pltpu.CompilerParams(
            dimension_semantics=("arbitrary",), vmem_limit_bytes=VMEM_LIMIT),
        name="merge_ln1_router",
    )(x2, ya, yb, proj, proj, pa, pb, wo, l1w, l1b, wrt, brt)


def _dispatch_kernel(dest_ref, h_ref, xs_in_hbm, xs_hbm, sem):
    del xs_in_hbm
    tm = h_ref.shape[0]

    def row_copy(t, k):
        return pltpu.make_async_copy(h_ref.at[pl.ds(t, 1)],
                                     xs_hbm.at[pl.ds(dest_ref[t * TOP_K + k], 1)], sem)

    def start(t, c):
        for k in range(TOP_K):
            row_copy(t, k).start()
        return c

    def wait(t, c):
        for k in range(TOP_K):
            row_copy(t, k).wait()
        return c

    lax.fori_loop(0, tm, start, 0, unroll=DMA_UNROLL)
    lax.fori_loop(0, tm, wait, 0, unroll=DMA_UNROLL)


def _dispatch(dest_flat, h, xs0):
    T, D = h.shape
    n = COMBINE_TM * TOP_K
    return pl.pallas_call(
        _dispatch_kernel,
        out_shape=jax.ShapeDtypeStruct(xs0.shape, xs0.dtype),
        grid=(T * TOP_K // n,),
        in_specs=[pl.BlockSpec((n,), lambda i: (i,), memory_space=pltpu.SMEM),
                  pl.BlockSpec((n // TOP_K, D), lambda i: (i, 0)),
                  pl.BlockSpec(memory_space=pl.ANY)],
        out_specs=pl.BlockSpec(memory_space=pl.ANY),
        scratch_shapes=[pltpu.SemaphoreType.DMA(())],
        input_output_aliases={2: 0},
        compiler_params=pltpu.CompilerParams(
            dimension_semantics=("arbitrary",), has_side_effects=True, vmem_limit_bytes=VMEM_LIMIT),
        name="moe_dispatch",
    )(dest_flat, h, xs0)


def _expert_kernel(be_ref, nu_ref, nv_ref, x_ref, wg_ref, wu_ref, bg_ref, bu_ref, wd_ref, bd_ref, o_ref, xb_ref):
    del be_ref
    i = pl.program_id(0)
    j = pl.program_id(1)
    used = i < nu_ref[0]
    half = o_ref.shape[0] // 2
    top_only = nv_ref[i] <= half

    @pl.when(jnp.logical_and(used, j == 0))
    def _():
        xb_ref[...] = x_ref[...].astype(bf16)

    @pl.when(jnp.logical_and(jnp.logical_not(used), j == 0))
    def _():
        o_ref[...] = jnp.zeros_like(o_ref)

    def ffn(rows, first):
        xb = xb_ref[rows, :]
        g = _dot(xb, wg_ref[0]) + bg_ref[0]
        u = _dot(xb, wu_ref[0]) + bu_ref[0]
        g = jnp.minimum(g, SWIGLU_LIMIT)
        u = jnp.clip(u, -SWIGLU_LIMIT, SWIGLU_LIMIT)
        act = (u + 1.0) * (g * jax.nn.sigmoid(SWIGLU_ALPHA * g))
        y = _dot(act.astype(bf16), wd_ref[0])
        if first:
            o_ref[rows, :] = y + bd_ref[0]
        else:
            o_ref[rows, :] += y

    for first in (True, False):
        at_step = (j == 0) if first else (j > 0)

        @pl.when(jnp.logical_and(jnp.logical_and(used, at_step), jnp.logical_not(top_only)))
        def _():
            ffn(slice(None), first)

        @pl.when(jnp.logical_and(jnp.logical_and(used, at_step), top_only))
        def _():
            ffn(slice(0, half), first)
            if first:
                o_ref[half:, :] = jnp.zeros((half, o_ref.shape[1]), o_ref.dtype)


def _experts(blk_e, nused, nvalid, xs, w_gu, b_gu, w_down, b_down):
    R, D = xs.shape
    E, _, F2 = w_gu.shape
    F = F2 // 2
    rb, tf = MOE_RB, min(MOE_TF, F)
    nf = F // tf

    def jj(i, j, nu):
        return jnp.where(i < nu[0], j, 0)

    grid_spec = pltpu.PrefetchScalarGridSpec(
        num_scalar_prefetch=3,
        grid=(R // rb, nf),
        in_specs=[pl.BlockSpec((rb, D), lambda i, j, be, nu, nv: (jnp.where(i < nu[0], i, 0), 0)),
                  pl.BlockSpec((1, D, tf), lambda i, j, be, nu, nv: (be[i], 0, jj(i, j, nu))),
                  pl.BlockSpec((1, D, tf), lambda i, j, be, nu, nv: (be[i], 0, nf + jj(i, j, nu))),
                  pl.BlockSpec((1, 1, tf), lambda i, j, be, nu, nv: (be[i], 0, jj(i, j, nu))),
                  pl.BlockSpec((1, 1, tf), lambda i, j, be, nu, nv: (be[i], 0, nf + jj(i, j, nu))),
                  pl.BlockSpec((1, tf, D), lambda i, j, be, nu, nv: (be[i], jj(i, j, nu), 0)),
                  pl.BlockSpec((1, 1, D), lambda i, j, be, nu, nv: (be[i], 0, 0))],
        out_specs=pl.BlockSpec((rb, D), lambda i, j, be, nu, nv: (i, 0)),
        scratch_shapes=[pltpu.VMEM((rb, D), bf16)],
    )
    return pl.pallas_call(
        _expert_kernel,
        out_shape=jax.ShapeDtypeStruct((R, D), f32),
        grid_spec=grid_spec,
        compiler_params=pltpu.CompilerParams(
            dimension_semantics=("parallel", "arbitrary"), vmem_limit_bytes=VMEM_LIMIT),
        name="moe_experts",
    )(blk_e, nused, nvalid, xs, w_gu, w_gu, b_gu, b_gu, w_down, b_down)


def _combine_kernel(dest_ref, dest_next_ref, h_ref, gate_ref, l2w_ref, l2b_ref, y_hbm, o_ref, buf, sem):
    i = pl.program_id(0)
    tm = h_ref.shape[0]
    slot = i % 2

    def row_copy(d_ref, sl, t, k):
        return pltpu.make_async_copy(y_hbm.at[pl.ds(d_ref[t * TOP_K + k], 1)],
                                     buf.at[sl, k, pl.ds(t, 1)], sem.at[sl])

    def start_tile(d_ref, sl):
        def body(t, c):
            for k in range(TOP_K):
                row_copy(d_ref, sl, t, k).start()
            return c
        lax.fori_loop(0, tm, body, 0, unroll=DMA_UNROLL)

    def wait_tile(d_ref, sl):
        def body(t, c):
            for k in range(TOP_K):
                row_copy(d_ref, sl, t, k).wait()
            return c
        lax.fori_loop(0, tm, body, 0, unroll=DMA_UNROLL)

    @pl.when(i == 0)
    def _():
        start_tile(dest_ref, 0)

    @pl.when(i + 1 < pl.num_programs(0))
    def _():
        start_tile(dest_next_ref, 1 - slot)

    wait_tile(dest_ref, slot)

    gate = gate_ref[...]
    ffn = gate[:, 0:1] * buf[slot, 0]
    for kslot in range(1, TOP_K):
        ffn = ffn + gate[:, kslot:kslot + 1] * buf[slot, kslot]
    o_ref[...] = _layer_norm(DEEPNORM_ALPHA * h_ref[...] + ffn, l2w_ref[...], l2b_ref[...])


def _combine(dest_flat, h, gate, l2w, l2b, y):
    T, D = h.shape
    tm = COMBINE_TM
    n = tm * TOP_K
    steps = T // tm
    return pl.pallas_call(
        _combine_kernel,
        out_shape=jax.ShapeDtypeStruct((T, D), f32),
        grid=(steps,),
        in_specs=[pl.BlockSpec((n,), lambda i: (i,), memory_space=pltpu.SMEM),
                  pl.BlockSpec((n,), lambda i: (jnp.minimum(i + 1, steps - 1),), memory_space=pltpu.SMEM),
                  pl.BlockSpec((tm, D), lambda i: (i, 0)),
                  pl.BlockSpec((tm, LANES), lambda i: (i, 0)),
                  pl.BlockSpec((1, D), lambda i: (0, 0)),
                  pl.BlockSpec((1, D), lambda i: (0, 0)),
                  pl.BlockSpec(memory_space=pl.ANY)],
        out_specs=pl.BlockSpec((tm, D), lambda i: (i, 0)),
        scratch_shapes=[pltpu.VMEM((2, TOP_K, tm, D), f32), pltpu.SemaphoreType.DMA((2,))],
        compiler_params=pltpu.CompilerParams(
            dimension_semantics=("arbitrary",), vmem_limit_bytes=VMEM_LIMIT),
        name="moe_combine_ln2",
    )(dest_flat, dest_flat, h, gate, l2w, l2b, y)


def _pack_w_in(w_in):
    o1 = 3 * ATT_WIDTH
    o2 = o1 + 3 * RWKV_WIDTH
    o3 = o2 + DECAY_LORA
    o4 = o3 + ICLR_LORA
    o5 = o4 + GATE_LORA
    padc = lambda w, n: jnp.pad(w, ((0, 0), (0, n - w.shape[1])))
    parts = [w_in[:, o5:], w_in[:, :o2], padc(w_in[:, o2:o3], LANES), padc(w_in[:, o3:o4], LANES),
             w_in[:, o4:o5]]
    return jnp.concatenate(parts, axis=1).astype(bf16)


def _token_mixing(x2, B, S, w_in, rel_bias, shift_mu, w0, w_up, a0, a_up, g_up, k_k, k_a, r_k,
                  lnx_w, lnx_b, side_a, side_b, zero_shape):
    T, D = x2.shape
    wp = _pack_w_in(w_in)
    proj = _in_proj(x2, wp, min(1024, T), 1536)

    y_a, side_a = _attention(proj, _attn_bias_table(rel_bias), B, S, side_a)

    W = RWKV_WIDTH
    row = lambda p: p.reshape(1, -1).astype(f32)
    padv = lambda p, n: jnp.pad(p, (0, n - p.shape[0]))
    mu_rkv = row(shift_mu[:3 * W])
    mu_lora = row(jnp.concatenate([padv(shift_mu[3 * W:3 * W + DECAY_LORA], LANES),
                                   padv(shift_mu[3 * W + DECAY_LORA:3 * W + DECAY_LORA + ICLR_LORA], LANES),
                                   shift_mu[3 * W + DECAY_LORA + ICLR_LORA:]]))
    padr = lambda w: jnp.pad(w, ((0, LANES - w.shape[0]), (0, 0))).astype(bf16)
    act = _lora_prep(proj, B, S, mu_lora)
    y_b, side_b, zeros = _rwkv(proj, act, B, S, mu_rkv, row(w0), row(a0), row(k_k), row(k_a), row(r_k),
                               row(lnx_w), row(lnx_b), padr(w_up), padr(a_up), g_up.astype(bf16), side_b,
                               zero_shape)
    return proj, y_a, y_b, side_a, side_b, zeros


def _moe_routing(topi, rank, counts, n_experts, rb, n_blocks):
    M = topi.shape[0] * TOP_K
    counts = counts.astype(i32)
    padded = (counts + rb - 1) // rb * rb
    pad_ends = jnp.cumsum(padded)
    pad_starts = pad_ends - padded
    onehot = topi[..., None] == jnp.arange(n_experts, dtype=i32)
    dest = (jnp.sum(jnp.where(onehot, pad_starts, 0), axis=-1) + rank).reshape(M)
    blk_start = jnp.arange(n_blocks, dtype=i32)[:, None] * rb
    blk_e = jnp.minimum(jnp.sum((pad_ends[None, :] <= blk_start).astype(i32), axis=-1), n_experts - 1)
    nused = (pad_ends[-1:] // rb).astype(i32)
    nvalid = jnp.clip((pad_starts + counts)[blk_e] - blk_start[:, 0], 0, rb).astype(i32)
    return dest.astype(i32), blk_e, nused, nvalid


def kernel(x, w_in, rel_bias, shift_mu, w0, w_up, a0, a_up, g_up, k_k, k_a, r_k, lnx_w, lnx_b,
           proj_a, proj_b, w_out, ln1_w, ln1_b, w_router, b_router, w_gu, b_gu, w_down, b_down,
           ln2_w, ln2_b):
    B, S, D = x.shape
    T = B * S
    h = x.reshape(T, D)
    for l in range(DEPTH):
        E, F = w_down.shape[1], w_down.shape[2]
        n_blocks = (T * TOP_K + E * (MOE_RB - 1) + MOE_RB - 1) // MOE_RB
        proj, y_a, y_b, w_down_b, w_gu_b, xs0 = _token_mixing(
            h, B, S, w_in[l], rel_bias[l], shift_mu[l], w0[l], w_up[l], a0[l], a_up[l], g_up[l],
            k_k[l], k_a[l], r_k[l], lnx_w[l], lnx_b[l],
            w_down[l].reshape(E * F, D), w_gu[l].reshape(E * D, 2 * F), (n_blocks * MOE_RB, D))
        h1, topi, gate, rank, counts = _merge_router(
            h, y_a, y_b, proj, proj_a[l].astype(bf16), proj_b[l].astype(bf16), w_out[l].astype(bf16),
            ln1_w[l].reshape(1, D), ln1_b[l].reshape(1, D), w_router[l].T, b_router[l].reshape(E, 1))
        gate = jnp.pad(gate[:TOP_K].T, ((0, 0), (0, LANES - TOP_K)))
        dest, blk_e, nused, nvalid = _moe_routing(
            topi[:TOP_K].T, rank[:TOP_K].T, counts[:, 0], E, MOE_RB, n_blocks)
        xs = _dispatch(dest, h1, xs0)
        y = _experts(blk_e, nused, nvalid, xs, w_gu_b.reshape(E, D, 2 * F), b_gu[l].reshape(E, 1, -1),
                     w_down_b.reshape(E, F, D), b_down[l].reshape(E, 1, D))
        h = _combine(dest, h1, gate, ln2_w[l].reshape(1, D), ln2_b[l].reshape(1, D), y)
    return h.reshape(B, S, D)
```

```python
import jax
import jax.numpy as jnp
from jax import lax
from jax.experimental import pallas as pl
from jax.experimental.pallas import tpu as pltpu

f32 = jnp.float32
bf16 = jnp.bfloat16
i32 = jnp.int32

CHUNK = 64
LEFT_CHUNKS = 8
HEAD_DIM = 64
ATT_WIDTH = 1024
MAX_PAST_DIST = 256
REL_TABLE = MAX_PAST_DIST + CHUNK
RWKV_WIDTH = 1024
DECAY_LORA = 96
ICLR_LORA = 96
GATE_LORA = 256
TOP_K = 4
SWIGLU_LIMIT = 7.0
SWIGLU_ALPHA = 1.702
LN_EPS = 1e-5
GN_EPS = 64e-5
DEPTH = 1
DEEPNORM_ALPHA = (2 * DEPTH) ** 0.25

LANES = 128
PAIR = 2 * HEAD_DIM
F32_SUBLANES = 8
BF16_SUBLANES = 16
VMEM_LIMIT = 56 * 1024 * 1024

NEG = -1e30

LORA_PAD = 512
COL_GATE = 0
COL_Q = 4096
COL_K = COL_Q + ATT_WIDTH
COL_V = COL_K + ATT_WIDTH
COL_R = COL_V + ATT_WIDTH
COL_KR = COL_R + RWKV_WIDTH
COL_VR = COL_KR + RWKV_WIDTH
COL_LORA = COL_VR + RWKV_WIDTH
PROJ_COLS = COL_LORA + LORA_PAD

ATT_TQ = 256
ATT_HB = 16
RWKV_TR = 1024
LORA_TR = 1024
MERGE_TM = 512
MOE_RB = 512
MOE_TF = 1024
COMBINE_TM = 256
DMA_UNROLL = 4


def _dot(a, b, precision=None):
    return jnp.dot(a, b, preferred_element_type=f32, precision=precision)


def _dot_nt(a, b):
    return lax.dot_general(a, b, (((1,), (1,)), ((), ())), preferred_element_type=f32)


def _dot_tn(a, b):
    return lax.dot_general(a, b, (((0,), (0,)), ((), ())), preferred_element_type=f32)


def _split_bf16(a):
    hi = a.astype(bf16)
    return hi, (a - hi.astype(f32)).astype(bf16)


def _dot_split_lhs(a, b_bf16):
    hi, lo = _split_bf16(a)
    return _dot(hi, b_bf16) + _dot(lo, b_bf16)


def _dot_split_rhs(a_bf16, b):
    hi, lo = _split_bf16(b)
    return _dot(a_bf16, hi) + _dot(a_bf16, lo)


def _inproj_kernel(x_ref, w_ref, o_ref, xb_ref):
    @pl.when(pl.program_id(1) == 0)
    def _():
        xb_ref[...] = x_ref[...].astype(bf16)

    o_ref[...] = _dot(xb_ref[...], w_ref[...]).astype(o_ref.dtype)


def _in_proj(x2, w, tm, tn):
    M, K = x2.shape
    N = w.shape[1]
    return pl.pallas_call(
        _inproj_kernel,
        out_shape=jax.ShapeDtypeStruct((M, N), bf16),
        grid=(M // tm, N // tn),
        in_specs=[pl.BlockSpec((tm, K), lambda i, j: (i, 0)),
                  pl.BlockSpec((K, tn), lambda i, j: (0, j))],
        out_specs=pl.BlockSpec((tm, tn), lambda i, j: (i, j)),
        scratch_shapes=[pltpu.VMEM((tm, K), bf16)],
        compiler_params=pltpu.CompilerParams(
            dimension_semantics=("parallel", "arbitrary"), vmem_limit_bytes=VMEM_LIMIT),
        name="in_proj",
    )(x2, w)


def _attn_bias_table(rel_bias):
    tq = ATT_TQ
    nk = 3 * tq
    H = rel_bias.shape[0]
    C = CHUNK
    nqc, nkc = tq // C, nk // C
    e = jnp.arange(-(nkc - 1), nqc)
    dist = (e[:, None] * C + jnp.arange(-(C - 1), C)[None, :]) + 2 * tq
    idx = jnp.clip(jnp.minimum(dist, MAX_PAST_DIST) + (C - 1), 0, REL_TABLE - 1)
    w = jnp.pad(rel_bias.astype(f32)[:, idx], ((0, 0), (0, 0), (0, 1)))
    ne = e.shape[0]
    G = jnp.broadcast_to(w[:, :, None, :], (H, ne, C, 2 * C)).reshape(H, ne, C * 2 * C)
    G = G[:, :, :C * (2 * C - 1)].reshape(H, ne, C, 2 * C - 1)
    blocks = jnp.swapaxes(G[:, :, :, C - 1:], 2, 3)
    qc = jnp.arange(nqc)[:, None]
    kc = jnp.arange(nkc)[None, :]
    d = qc + (2 * tq) // C - kc
    tiles = blocks[:, (qc - kc) + (nkc - 1)]
    tiles = jnp.where(((d >= 0) & (d <= LEFT_CHUNKS))[None, :, :, None, None], tiles, NEG)
    return jnp.transpose(tiles, (0, 1, 3, 2, 4)).reshape(H, tq, nk)


def _attn_kernel(q_ref, k0_ref, k1_ref, k2_ref, v0_ref, v1_ref, v2_ref, bias_ref, side_ref,
                 o_ref, side_o_ref):
    qt = pl.program_id(2)
    side_o_ref[...] = side_ref[...].astype(side_o_ref.dtype)
    tq = q_ref.shape[0]
    k_refs = (k0_ref, k1_ref, k2_ref)
    v_refs = (v0_ref, v1_ref, v2_ref)
    heads = range(ATT_HB)
    blocks = range(3)
    lane = lax.broadcasted_iota(i32, (1, PAIR), 1)
    head0 = lane < HEAD_DIM
    pair = [slice((h // 2) * PAIR, (h // 2 + 1) * PAIR) for h in heads]
    q = [jnp.where(head0 if h % 2 == 0 else jnp.logical_not(head0),
                   q_ref[:, pair[h]] * (HEAD_DIM ** -0.5), 0.0).astype(bf16) for h in heads]
    s = [[_dot_nt(q[h], k_refs[j][:, pair[h]]) + bias_ref[h, :, j * tq:(j + 1) * tq] for j in blocks]
         for h in heads]
    s = [[jnp.where(qt - 2 + j >= 0, s[h][j], NEG) if j < 2 else s[h][j] for j in blocks] for h in heads]
    m = [jnp.max(jnp.maximum(jnp.maximum(s[h][0], s[h][1]), s[h][2]), axis=-1, keepdims=True)
         for h in heads]
    p = [[jnp.exp((s[h][j] - m[h]).astype(bf16)) for j in blocks] for h in heads]
    ones = jnp.ones((tq, PAIR), bf16)
    v1 = [[jnp.concatenate([v_refs[j][:, pair[2 * g]], ones], axis=1) for j in blocks]
          for g in range(ATT_HB // 2)]
    acc = [_dot(p[h][0], v1[h // 2][0]) + _dot(p[h][1], v1[h // 2][1]) + _dot(p[h][2], v1[h // 2][2])
           for h in heads]
    out = [acc[h][:, 0:PAIR] / acc[h][:, PAIR:PAIR + 1] for h in heads]
    for g in range(ATT_HB // 2):
        o_ref[:, pair[2 * g]] = jnp.where(head0, out[2 * g], out[2 * g + 1]).astype(o_ref.dtype)


def _attention(proj, bias, B, S, side):
    T = B * S
    tq = ATT_TQ
    nt = S // tq
    width = ATT_HB * HEAD_DIM
    ngrp = ATT_WIDTH // width
    side_spec = pl.BlockSpec((side.shape[0] // (B * nt * ngrp), side.shape[1]),
                             lambda h, b, t: ((h * B + b) * nt + t, 0))
    qcol, kcol, vcol = COL_Q // width, COL_K // width, COL_V // width

    def kv_spec(col, back):
        return pl.BlockSpec((tq, width),
                            lambda h, b, t: (b * nt + jnp.maximum(t - back, 0), col + h))

    return pl.pallas_call(
        _attn_kernel,
        out_shape=(jax.ShapeDtypeStruct((T, ATT_WIDTH), bf16), jax.ShapeDtypeStruct(side.shape, bf16)),
        grid=(ngrp, B, nt),
        in_specs=[pl.BlockSpec((tq, width), lambda h, b, t: (b * nt + t, qcol + h)),
                  kv_spec(kcol, 2), kv_spec(kcol, 1), kv_spec(kcol, 0),
                  kv_spec(vcol, 2), kv_spec(vcol, 1), kv_spec(vcol, 0),
                  pl.BlockSpec((ATT_HB, tq, 3 * tq), lambda h, b, t: (h, 0, 0)),
                  side_spec],
        out_specs=(pl.BlockSpec((tq, width), lambda h, b, t: (b * nt + t, h)), side_spec),
        compiler_params=pltpu.CompilerParams(
            dimension_semantics=("parallel", "parallel", "parallel"), vmem_limit_bytes=VMEM_LIMIT),
        name="chunk_attn",
    )(proj, proj, proj, proj, proj, proj, proj, bias, side)


def _lora_prep_kernel(lo_ref, lop_ref, mu_ref, o_ref):
    t = pl.program_id(1)
    tr = lo_ref.shape[0]
    p = lo_ref[...].astype(f32)
    prev_row = lop_ref[BF16_SUBLANES - 1:BF16_SUBLANES, :].astype(f32) * jnp.where(t == 0, 0.0, 1.0)
    row = lax.broadcasted_iota(i32, (tr, 1), 0)
    prev = jnp.where(row == 0, prev_row, pltpu.roll(p, 1, 0))
    lo = p + (prev - p) * mu_ref[...]
    o_ref[:, 0:LANES] = jnp.tanh(lo[:, 0:LANES]).astype(o_ref.dtype)
    o_ref[:, LANES:2 * LANES] = lo[:, LANES:2 * LANES].astype(o_ref.dtype)
    o_ref[:, 2 * LANES:] = jax.nn.sigmoid(lo[:, 2 * LANES:]).astype(o_ref.dtype)


def _lora_prep(proj, B, S, mu_lora):
    tr = LORA_TR
    nt = S // tr
    sub = tr // BF16_SUBLANES
    nsub = S // BF16_SUBLANES
    locol = COL_LORA // LORA_PAD
    return pl.pallas_call(
        _lora_prep_kernel,
        out_shape=jax.ShapeDtypeStruct((B * S, LORA_PAD), bf16),
        grid=(B, nt),
        in_specs=[pl.BlockSpec((tr, LORA_PAD), lambda b, t: (b * nt + t, locol)),
                  pl.BlockSpec((BF16_SUBLANES, LORA_PAD),
                               lambda b, t: (b * nsub + jnp.maximum(t * sub - 1, 0), locol)),
                  pl.BlockSpec((1, LORA_PAD), lambda b, t: (0, 0))],
        out_specs=pl.BlockSpec((tr, LORA_PAD), lambda b, t: (b * nt + t, 0)),
        compiler_params=pltpu.CompilerParams(
            dimension_semantics=("parallel", "parallel"), vmem_limit_bytes=VMEM_LIMIT),
        name="rwkv7_lora_prep",
    )(proj, proj, mu_lora)


def _rwkv_kernel(r_ref, k_ref, v_ref, act_ref, rp_ref, kp_ref, vp_ref,
                 mur_ref, muk_ref, muv_ref, w0_ref, a0_ref, kkp_ref, kap_ref, rkp_ref,
                 lnw_ref, lnb_ref, wup_ref, aup_ref, gup_ref, side_ref,
                 o_ref, side_o_ref, zero_o_ref,
                 st_ref, r_s, lw_s, k_s, v_s, a_s, b_s, g_s, bon_s):
    t = pl.program_id(2)
    tr = r_ref.shape[0]
    C = CHUNK

    side_o_ref[...] = side_ref[...].astype(side_o_ref.dtype)
    zero_o_ref[...] = jnp.zeros_like(zero_o_ref)

    @pl.when(t == 0)
    def _():
        st_ref[...] = jnp.zeros_like(st_ref)

    not_first = jnp.where(t == 0, 0.0, 1.0).astype(f32)
    row = lax.broadcasted_iota(i32, (tr, 1), 0)

    def shift(p_ref, prev_ref, mu_ref):
        p = p_ref[...].astype(f32)
        prev_row = prev_ref[BF16_SUBLANES - 1:BF16_SUBLANES, :].astype(f32) * not_first
        prev = jnp.where(row == 0, prev_row, pltpu.roll(p, 1, 0))
        return p + (prev - p) * mu_ref[...]

    r = shift(r_ref, rp_ref, mur_ref)
    k = shift(k_ref, kp_ref, muk_ref)
    v = shift(v_ref, vp_ref, muv_ref)

    ri = lax.broadcasted_iota(i32, (PAIR, PAIR), 0)
    ci = lax.broadcasted_iota(i32, (PAIR, PAIR), 1)
    same_head = (ri // HEAD_DIM) == (ci // HEAD_DIM)
    head_ones = jnp.where(same_head, 1.0, 0.0).astype(bf16)
    head_avg = jnp.where(same_head, 1.0 / HEAD_DIM, 0.0).astype(bf16)

    z = w0_ref[...] + _dot(act_ref[:, 0:LANES], wup_ref[...])
    softplus_negz = jnp.maximum(-z, 0.0) + jnp.log(1.0 + jnp.exp(-jnp.abs(z)))
    lw = -jnp.exp(-softplus_negz - 0.5)
    alr = jax.nn.sigmoid(a0_ref[...] + _dot(act_ref[:, LANES:2 * LANES], aup_ref[...]))
    g = _dot(act_ref[:, 2 * LANES:], gup_ref[...])
    kk = k * kkp_ref[...]
    kk = kk / jnp.maximum(jnp.sqrt(_dot_split_lhs(kk * kk, head_ones)), 1e-12)
    k2 = k * (1.0 + (alr - 1.0) * kap_ref[...])
    bonus = _dot_split_lhs(r * k2 * rkp_ref[...], head_ones) * v

    r_s[...] = r
    lw_s[...] = lw
    k_s[...] = k2
    v_s[...] = v
    a_s[...] = -kk
    b_s[...] = kk * alr
    g_s[...] = g
    bon_s[...] = bonus

    lane = lax.broadcasted_iota(i32, (1, PAIR), 1)
    head0 = lane < HEAD_DIM
    tri_incl = jnp.where(lax.broadcasted_iota(i32, (C, C), 1) <= lax.broadcasted_iota(i32, (C, C), 0),
                         1.0, 0.0).astype(bf16)
    strict = ci < ri
    incl = ci <= ri
    eye = ci == ri
    lnw = lnw_ref[...]
    lnb = lnb_ref[...]

    def stack(x):
        return jnp.concatenate([jnp.where(head0, x, 0.0), jnp.where(head0, 0.0, x)], axis=0)

    nc = tr // C
    chunks = range(nc)
    rows = [slice(c * C, (c + 1) * C) for c in chunks]
    lwc = [lw_s[rw, :] for rw in rows]
    L = [_dot_split_rhs(tri_incl, x) for x in lwc]
    LC = [x[C - 1:C, :] for x in L]
    a_t, r_t, r_tb, bk_t, v_st = [], [], [], [], []
    for c in chunks:
        rw = rows[c]
        e_neg = jnp.exp(-L[c])
        a_t.append(stack(a_s[rw, :] * jnp.exp(L[c] - lwc[c])).astype(bf16))
        r_t.append(stack(r_s[rw, :] * jnp.exp(L[c])))
        r_tb.append(r_t[c].astype(bf16))
        bk_t.append(jnp.concatenate([stack(b_s[rw, :] * e_neg), stack(k_s[rw, :] * e_neg)], axis=0).astype(bf16))
        v_st.append(stack(v_s[rw, :]).astype(bf16))

    A = [_dot_nt(jnp.concatenate([a_t[c], r_tb[c]], axis=0), bk_t[c]) for c in chunks]
    a_ab = [jnp.where(strict, x[0:PAIR, 0:PAIR], 0.0) for x in A]
    a_kk = [jnp.concatenate([jnp.where(strict, x[0:PAIR, PAIR:], 0.0),
                             jnp.where(incl, x[PAIR:, PAIR:], 0.0)], axis=0).astype(bf16) for x in A]
    a_rb = [jnp.where(incl, x[PAIR:, 0:PAIR], 0.0).astype(bf16) for x in A]

    def sub_diag(s):
        return jnp.logical_and(jnp.logical_and((ri // (2 * s)) == (ci // (2 * s)), (ri // s) % 2 == 1),
                               (ci // s) % 2 == 0)

    tinv = [jnp.where(eye, 1.0, 0.0).astype(f32) + jnp.where(sub_diag(1), x, 0.0) for x in a_ab]
    s_blk = 2
    while s_blk < HEAD_DIM:
        mask = sub_diag(s_blk)
        n21 = [jnp.where(mask, x, 0.0).astype(bf16) for x in a_ab]
        tb = [tv.astype(bf16) for tv in tinv]
        n21_t11 = [_dot(x, tv).astype(bf16) for x, tv in zip(n21, tb)]
        tinv = [tv + _dot(tvb, x) for tv, tvb, x in zip(tinv, tb, n21_t11)]
        s_blk *= 2

    av = [_dot(a_kk[c], v_st[c]) for c in chunks]
    wu = [_dot(tinv[c].astype(bf16), jnp.concatenate([a_t[c], av[c][0:PAIR].astype(bf16)], axis=1))
          for c in chunks]
    wub = [x.astype(bf16) for x in wu]
    ry = [jnp.concatenate([r_t[c], av[c][PAIR:]], axis=1) + _dot(a_rb[c], wub[c]) for c in chunks]
    ry = [x[0:C] + x[C:] for x in ry]

    mct, gct = [], []
    for c in chunks:
        rw = rows[c]
        e_end = jnp.exp(LC[c] - L[c])
        b_h = stack(b_s[rw, :] * e_end).astype(bf16)
        k_h = stack(k_s[rw, :] * e_end).astype(bf16)
        mct.append(jnp.where(eye, jnp.exp(LC[c]), 0.0) + _dot_tn(b_h, wub[c][:, 0:PAIR]))
        gct.append(_dot_tn(jnp.concatenate([b_h, k_h], axis=0),
                           jnp.concatenate([wub[c][:, PAIR:], v_st[c]], axis=0)))

    st = st_ref[...]
    y = []
    for c in chunks:
        stb = st.astype(bf16)
        y.append(_dot(ry[c][:, 0:PAIR].astype(bf16), stb) + ry[c][:, PAIR:])
        st = _dot(mct[c].astype(bf16), stb) + gct[c]
    st_ref[...] = st

    mu = [_dot(x.astype(bf16), head_avg) for x in y]
    d = [x - m for x, m in zip(y, mu)]
    var = [_dot((x * x).astype(bf16), head_avg) for x in d]
    for c in chunks:
        yn = d[c] * lax.rsqrt(var[c] + GN_EPS) * lnw + lnb
        o_ref[rows[c], :] = ((yn + bon_s[rows[c], :]) * g_s[rows[c], :]).astype(o_ref.dtype)


def _rwkv(proj, act, B, S, mu_rkv, w0, a0, k_k, k_a, r_k, lnx_w, lnx_b, w_up_p, a_up_p, g_up, side,
          zero_shape):
    T = B * S
    tr = RWKV_TR
    nt = S // tr
    sub = tr // BF16_SUBLANES
    nsub = S // BF16_SUBLANES
    npair = RWKV_WIDTH // PAIR
    rcol, kcol, vcol = COL_R // PAIR, COL_KR // PAIR, COL_VR // PAIR
    side_rows = side.shape[0] // (B * npair * nt)
    zero_rows = zero_shape[0] // (B * npair * nt)
    step = lambda b, h, t: ((b * npair + h) * nt + t, 0)

    def cur(col):
        return pl.BlockSpec((tr, PAIR), lambda b, h, t: (b * nt + t, col + h))

    def prev(col):
        return pl.BlockSpec((BF16_SUBLANES, PAIR),
                            lambda b, h, t: (b * nsub + jnp.maximum(t * sub - 1, 0), col + h))

    def vec(off):
        return pl.BlockSpec((1, PAIR), lambda b, h, t: (0, off + h))

    in_specs = [
        cur(rcol), cur(kcol), cur(vcol),
        pl.BlockSpec((tr, LORA_PAD), lambda b, h, t: (b * nt + t, 0)),
        prev(rcol), prev(kcol), prev(vcol),
        vec(0), vec(npair), vec(2 * npair),
        vec(0), vec(0), vec(0), vec(0), vec(0), vec(0), vec(0),
        pl.BlockSpec((LANES, PAIR), lambda b, h, t: (0, h)),
        pl.BlockSpec((LANES, PAIR), lambda b, h, t: (0, h)),
        pl.BlockSpec((GATE_LORA, PAIR), lambda b, h, t: (0, h)),
        pl.BlockSpec((side_rows, side.shape[1]), step),
    ]
    scratch = [pltpu.VMEM((PAIR, PAIR), f32)] + [pltpu.VMEM((tr, PAIR), f32)] * 8
    return pl.pallas_call(
        _rwkv_kernel,
        out_shape=(jax.ShapeDtypeStruct((T, RWKV_WIDTH), bf16),
                   jax.ShapeDtypeStruct(side.shape, bf16),
                   jax.ShapeDtypeStruct(zero_shape, f32)),
        grid=(B, npair, nt),
        in_specs=in_specs,
        out_specs=(pl.BlockSpec((tr, PAIR), lambda b, h, t: (b * nt + t, h)),
                   pl.BlockSpec((side_rows, side.shape[1]), step),
                   pl.BlockSpec((zero_rows, zero_shape[1]), step)),
        scratch_shapes=scratch,
        compiler_params=pltpu.CompilerParams(
            dimension_semantics=("parallel", "parallel", "arbitrary"), vmem_limit_bytes=VMEM_LIMIT),
        name="rwkv7",
    )(proj, proj, proj, act, proj, proj, proj,
      mu_rkv, mu_rkv, mu_rkv, w0, a0, k_k, k_a, r_k, lnx_w, lnx_b, w_up_p, a_up_p, g_up, side)


def _layer_norm(z, w, b):
    mu = jnp.mean(z, axis=-1, keepdims=True)
    d = z - mu
    var = jnp.mean(d * d, axis=-1, keepdims=True)
    return d * lax.rsqrt(var + LN_EPS) * w + b


def _merge_kernel(x_ref, ya_ref, yb_ref, ga_ref, gb_ref, pa_ref, pb_ref, wo_ref, l1w_ref, l1b_ref,
                  wrt_ref, brt_ref,
                  h_ref, topi_ref, gate_ref, rank_ref, cnt_ref,
                  run_ref):
    i = pl.program_id(0)
    tm = x_ref.shape[0]
    ne = wrt_ref.shape[0]

    @pl.when(i == 0)
    def _():
        run_ref[...] = jnp.zeros_like(run_ref)

    halves = [slice(0, tm // 2), slice(tm // 2, tm)]
    ma = [_dot(ya_ref[rw, :], pa_ref[...]) for rw in halves]
    mb = [_dot(yb_ref[rw, :], pb_ref[...]) for rw in halves]
    merged = [(jax.nn.sigmoid(ga_ref[rw, :].astype(f32)) * a
               + jax.nn.sigmoid(gb_ref[rw, :].astype(f32)) * b).astype(bf16)
              for rw, a, b in zip(halves, ma, mb)]
    mix = [_dot(m, wo_ref[...]) for m in merged]
    h = jnp.concatenate([_layer_norm(DEEPNORM_ALPHA * x_ref[rw, :] + mx, l1w_ref[...], l1b_ref[...])
                         for rw, mx in zip(halves, mix)], axis=0)
    h_ref[...] = h

    h_hi, h_lo = _split_bf16(h)
    w_hi, w_lo = _split_bf16(wrt_ref[...])
    prod = _dot_nt(jnp.concatenate([w_hi, w_lo], axis=0), jnp.concatenate([h_hi, h_lo], axis=0))
    lg = prod[0:ne, 0:tm] + (prod[0:ne, tm:] + prod[ne:, 0:tm]) + brt_ref[...]
    eidx = lax.broadcasted_iota(i32, (ne, tm), 0)
    slot = lax.broadcasted_iota(i32, (F32_SUBLANES, tm), 0)
    vals, idxs = [], []
    for _ in range(TOP_K):
        m = jnp.max(lg, axis=0, keepdims=True)
        idx = jnp.min(jnp.where(lg == m, eidx, ne), axis=0, keepdims=True)
        vals.append(m)
        idxs.append(idx)
        lg = jnp.where(eidx == idx, -jnp.inf, lg)
    exps = [jnp.exp(vk - vals[0]) for vk in vals]
    denom = exps[0] + exps[1] + exps[2] + exps[3]

    onehots = [jnp.where(eidx == idx, 1.0, 0.0).astype(f32) for idx in idxs]
    oh = onehots[0] + onehots[1] + onehots[2] + onehots[3]
    tri = jnp.where(lax.broadcasted_iota(i32, (tm, tm), 0) < lax.broadcasted_iota(i32, (tm, tm), 1),
                    1.0, 0.0).astype(bf16)
    before = run_ref[...] + _dot(oh.astype(bf16), tri)

    topi_o = jnp.zeros((F32_SUBLANES, tm), i32)
    gate_o = jnp.zeros((F32_SUBLANES, tm), f32)
    rank_o = jnp.zeros((F32_SUBLANES, tm), i32)
    for kslot in range(TOP_K):
        rk = jnp.sum(onehots[kslot] * before, axis=0, keepdims=True).astype(i32)
        topi_o = jnp.where(slot == kslot, idxs[kslot], topi_o)
        gate_o = jnp.where(slot == kslot, exps[kslot] / denom, gate_o)
        rank_o = jnp.where(slot == kslot, rk, rank_o)
    topi_ref[...] = topi_o
    gate_ref[...] = gate_o
    rank_ref[...] = rank_o
    run = run_ref[...] + jnp.sum(oh, axis=1, keepdims=True)
    run_ref[...] = run
    cnt_ref[...] = run


def _merge_router(x2, ya, yb, proj, pa, pb, wo, l1w, l1b, wrt, brt):
    T, D = x2.shape
    tm = MERGE_TM
    ne = wrt.shape[0]
    gcol = COL_GATE // D

    def const(shape):
        return pl.BlockSpec(shape, lambda i: (0, 0), pipeline_mode=pl.Buffered(1))

    out_shape = (jax.ShapeDtypeStruct((T, D), f32),
                 jax.ShapeDtypeStruct((F32_SUBLANES, T), i32),
                 jax.ShapeDtypeStruct((F32_SUBLANES, T), f32),
                 jax.ShapeDtypeStruct((F32_SUBLANES, T), i32),
                 jax.ShapeDtypeStruct((ne, 1), f32))
    return pl.pallas_call(
        _merge_kernel,
        out_shape=out_shape,
        grid=(T // tm,),
        in_specs=[pl.BlockSpec((tm, D), lambda i: (i, 0)),
                  pl.BlockSpec((tm, ATT_WIDTH), lambda i: (i, 0)),
                  pl.BlockSpec((tm, RWKV_WIDTH), lambda i: (i, 0)),
                  pl.BlockSpec((tm, D), lambda i: (i, gcol)),
                  pl.BlockSpec((tm, D), lambda i: (i, gcol + 1)),
                  const(pa.shape), const(pb.shape), const(wo.shape),
                  const((1, D)), const((1, D)), const(wrt.shape), const((ne, 1))],
        out_specs=(pl.BlockSpec((tm, D), lambda i: (i, 0)),
                   pl.BlockSpec((F32_SUBLANES, tm), lambda i: (0, i)),
                   pl.BlockSpec((F32_SUBLANES, tm), lambda i: (0, i)),
                   pl.BlockSpec((F32_SUBLANES, tm), lambda i: (0, i)),
                   pl.BlockSpec((ne, 1), lambda i: (0, 0))),
        scratch_shapes=[pltpu.VMEM((ne, 1), f32)],
        compiler_params=pltpu.CompilerParams(
            dimension_semantics=("arbitrary",), vmem_limit_bytes=VMEM_LIMIT),
        name="merge_ln1_router",
    )(x2, ya, yb, proj, proj, pa, pb, wo, l1w, l1b, wrt, brt)


def _dispatch_kernel(dest_ref, h_ref, xs_in_hbm, xs_hbm, sem):
    del xs_in_hbm
    tm = h_ref.shape[0]

    def row_copy(t, k):
        return pltpu.make_async_copy(h_ref.at[pl.ds(t, 1)],
                                     xs_hbm.at[pl.ds(dest_ref[t * TOP_K + k], 1)], sem)

    def start(t, c):
        for k in range(TOP_K):
            row_copy(t, k).start()
        return c

    def wait(t, c):
        for k in range(TOP_K):
            row_copy(t, k).wait()
        return c

    lax.fori_loop(0, tm, start, 0, unroll=DMA_UNROLL)
    lax.fori_loop(0, tm, wait, 0, unroll=DMA_UNROLL)


def _dispatch(dest_flat, h, xs0):
    T, D = h.shape
    n = COMBINE_TM * TOP_K
    return pl.pallas_call(
        _dispatch_kernel,
        out_shape=jax.ShapeDtypeStruct(xs0.shape, xs0.dtype),
        grid=(T * TOP_K // n,),
        in_specs=[pl.BlockSpec((n,), lambda i: (i,), memory_space=pltpu.SMEM),
                  pl.BlockSpec((n // TOP_K, D), lambda i: (i, 0)),
                  pl.BlockSpec(memory_space=pl.ANY)],
        out_specs=pl.BlockSpec(memory_space=pl.ANY),
        scratch_shapes=[pltpu.SemaphoreType.DMA(())],
        input_output_aliases={2: 0},
        compiler_params=pltpu.CompilerParams(
            dimension_semantics=("arbitrary",), has_side_effects=True, vmem_limit_bytes=VMEM_LIMIT),
        name="moe_dispatch",
    )(dest_flat, h, xs0)


def _expert_kernel(be_ref, nu_ref, nv_ref, x_ref, wg_ref, wu_ref, bg_ref, bu_ref, wd_ref, bd_ref, o_ref, xb_ref):
    del be_ref
    i = pl.program_id(0)
    j = pl.program_id(1)
    used = i < nu_ref[0]
    half = o_ref.shape[0] // 2
    top_only = nv_ref[i] <= half

    @pl.when(jnp.logical_and(used, j == 0))
    def _():
        xb_ref[...] = x_ref[...].astype(bf16)

    @pl.when(jnp.logical_and(jnp.logical_not(used), j == 0))
    def _():
        o_ref[...] = jnp.zeros_like(o_ref)

    def ffn(rows, first):
        xb = xb_ref[rows, :]
        g = _dot(xb, wg_ref[0]) + bg_ref[0]
        u = _dot(xb, wu_ref[0]) + bu_ref[0]
        g = jnp.minimum(g, SWIGLU_LIMIT)
        u = jnp.clip(u, -SWIGLU_LIMIT, SWIGLU_LIMIT)
        act = (u + 1.0) * (g * jax.nn.sigmoid(SWIGLU_ALPHA * g))
        y = _dot(act.astype(bf16), wd_ref[0])
        if first:
            o_ref[rows, :] = y + bd_ref[0]
        else:
            o_ref[rows, :] += y

    for first in (True, False):
        at_step = (j == 0) if first else (j > 0)

        @pl.when(jnp.logical_and(jnp.logical_and(used, at_step), jnp.logical_not(top_only)))
        def _():
            ffn(slice(None), first)

        @pl.when(jnp.logical_and(jnp.logical_and(used, at_step), top_only))
        def _():
            ffn(slice(0, half), first)
            if first:
                o_ref[half:, :] = jnp.zeros((half, o_ref.shape[1]), o_ref.dtype)


def _experts(blk_e, nused, nvalid, xs, w_gu, b_gu, w_down, b_down):
    R, D = xs.shape
    E, _, F2 = w_gu.shape
    F = F2 // 2
    rb, tf = MOE_RB, min(MOE_TF, F)
    nf = F // tf

    def jj(i, j, nu):
        return jnp.where(i < nu[0], j, 0)

    grid_spec = pltpu.PrefetchScalarGridSpec(
        num_scalar_prefetch=3,
        grid=(R // rb, nf),
        in_specs=[pl.BlockSpec((rb, D), lambda i, j, be, nu, nv: (jnp.where(i < nu[0], i, 0), 0)),
                  pl.BlockSpec((1, D, tf), lambda i, j, be, nu, nv: (be[i], 0, jj(i, j, nu))),
                  pl.BlockSpec((1, D, tf), lambda i, j, be, nu, nv: (be[i], 0, nf + jj(i, j, nu))),
                  pl.BlockSpec((1, 1, tf), lambda i, j, be, nu, nv: (be[i], 0, jj(i, j, nu))),
                  pl.BlockSpec((1, 1, tf), lambda i, j, be, nu, nv: (be[i], 0, nf + jj(i, j, nu))),
                  pl.BlockSpec((1, tf, D), lambda i, j, be, nu, nv: (be[i], jj(i, j, nu), 0)),
                  pl.BlockSpec((1, 1, D), lambda i, j, be, nu, nv: (be[i], 0, 0))],
        out_specs=pl.BlockSpec((rb, D), lambda i, j, be, nu, nv: (i, 0)),
        scratch_shapes=[pltpu.VMEM((rb, D), bf16)],
    )
    return pl.pallas_call(
        _expert_kernel,
        out_shape=jax.ShapeDtypeStruct((R, D), f32),
        grid_spec=grid_spec,
        compiler_params=pltpu.CompilerParams(
            dimension_semantics=("parallel", "arbitrary"), vmem_limit_bytes=VMEM_LIMIT),
        name="moe_experts",
    )(blk_e, nused, nvalid, xs, w_gu, w_gu, b_gu, b_gu, w_down, b_down)


def _combine_kernel(dest_ref, dest_next_ref, h_ref, gate_ref, l2w_ref, l2b_ref, y_hbm, o_ref, buf, sem):
    i = pl.program_id(0)
    tm = h_ref.shape[0]
    slot = i % 2

    def row_copy(d_ref, sl, t, k):
        return pltpu.make_async_copy(y_hbm.at[pl.ds(d_ref[t * TOP_K + k], 1)],
                                     buf.at[sl, k, pl.ds(t, 1)], sem.at[sl])

    def start_tile(d_ref, sl):
        def body(t, c):
            for k in range(TOP_K):
                row_copy(d_ref, sl, t, k).start()
            return c
        lax.fori_loop(0, tm, body, 0, unroll=DMA_UNROLL)

    def wait_tile(d_ref, sl):
        def body(t, c):
            for k in range(TOP_K):
                row_copy(d_ref, sl, t, k).wait()
            return c
        lax.fori_loop(0, tm, body, 0, unroll=DMA_UNROLL)

    @pl.when(i == 0)
    def _():
        start_tile(dest_ref, 0)

    @pl.when(i + 1 < pl.num_programs(0))
    def _():
        start_tile(dest_next_ref, 1 - slot)

    wait_tile(dest_ref, slot)

    gate = gate_ref[...]
    ffn = gate[:, 0:1] * buf[slot, 0]
    for kslot in range(1, TOP_K):
        ffn = ffn + gate[:, kslot:kslot + 1] * buf[slot, kslot]
    o_ref[...] = _layer_norm(DEEPNORM_ALPHA * h_ref[...] + ffn, l2w_ref[...], l2b_ref[...])


def _combine(dest_flat, h, gate, l2w, l2b, y):
    T, D = h.shape
    tm = COMBINE_TM
    n = tm * TOP_K
    steps = T // tm
    return pl.pallas_call(
        _combine_kernel,
        out_shape=jax.ShapeDtypeStruct((T, D), f32),
        grid=(steps,),
        in_specs=[pl.BlockSpec((n,), lambda i: (i,), memory_space=pltpu.SMEM),
                  pl.BlockSpec((n,), lambda i: (jnp.minimum(i + 1, steps - 1),), memory_space=pltpu.SMEM),
                  pl.BlockSpec((tm, D), lambda i: (i, 0)),
                  pl.BlockSpec((tm, LANES), lambda i: (i, 0)),
                  pl.BlockSpec((1, D), lambda i: (0, 0)),
                  pl.BlockSpec((1, D), lambda i: (0, 0)),
                  pl.BlockSpec(memory_space=pl.ANY)],
        out_specs=pl.BlockSpec((tm, D), lambda i: (i, 0)),
        scratch_shapes=[pltpu.VMEM((2, TOP_K, tm, D), f32), pltpu.SemaphoreType.DMA((2,))],
        compiler_params=pltpu.CompilerParams(
            dimension_semantics=("arbitrary",), vmem_limit_bytes=VMEM_LIMIT),
        name="moe_combine_ln2",
    )(dest_flat, dest_flat, h, gate, l2w, l2b, y)


def _pack_w_in(w_in):
    o1 = 3 * ATT_WIDTH
    o2 = o1 + 3 * RWKV_WIDTH
    o3 = o2 + DECAY_LORA
    o4 = o3 + ICLR_LORA
    o5 = o4 + GATE_LORA
    padc = lambda w, n: jnp.pad(w, ((0, 0), (0, n - w.shape[1])))
    parts = [w_in[:, o5:], w_in[:, :o2], padc(w_in[:, o2:o3], LANES), padc(w_in[:, o3:o4], LANES),
             w_in[:, o4:o5]]
    return jnp.concatenate(parts, axis=1).astype(bf16)


def _token_mixing(x2, B, S, w_in, rel_bias, shift_mu, w0, w_up, a0, a_up, g_up, k_k, k_a, r_k,
                  lnx_w, lnx_b, side_a, side_b, zero_shape):
    T, D = x2.shape
    wp = _pack_w_in(w_in)
    proj = _in_proj(x2, wp, min(1024, T), 1536)

    y_a, side_a = _attention(proj, _attn_bias_table(rel_bias), B, S, side_a)

    W = RWKV_WIDTH
    row = lambda p: p.reshape(1, -1).astype(f32)
    padv = lambda p, n: jnp.pad(p, (0, n - p.shape[0]))
    mu_rkv = row(shift_mu[:3 * W])
    mu_lora = row(jnp.concatenate([padv(shift_mu[3 * W:3 * W + DECAY_LORA], LANES),
                                   padv(shift_mu[3 * W + DECAY_LORA:3 * W + DECAY_LORA + ICLR_LORA], LANES),
                                   shift_mu[3 * W + DECAY_LORA + ICLR_LORA:]]))
    padr = lambda w: jnp.pad(w, ((0, LANES - w.shape[0]), (0, 0))).astype(bf16)
    act = _lora_prep(proj, B, S, mu_lora)
    y_b, side_b, zeros = _rwkv(proj, act, B, S, mu_rkv, row(w0), row(a0), row(k_k), row(k_a), row(r_k),
                               row(lnx_w), row(lnx_b), padr(w_up), padr(a_up), g_up.astype(bf16), side_b,
                               zero_shape)
    return proj, y_a, y_b, side_a, side_b, zeros


def _moe_routing(topi, rank, counts, n_experts, rb, n_blocks):
    M = topi.shape[0] * TOP_K
    counts = counts.astype(i32)
    padded = (counts + rb - 1) // rb * rb
    pad_ends = jnp.cumsum(padded)
    pad_starts = pad_ends - padded
    onehot = topi[..., None] == jnp.arange(n_experts, dtype=i32)
    dest = (jnp.sum(jnp.where(onehot, pad_starts, 0), axis=-1) + rank).reshape(M)
    blk_start = jnp.arange(n_blocks, dtype=i32)[:, None] * rb
    blk_e = jnp.minimum(jnp.sum((pad_ends[None, :] <= blk_start).astype(i32), axis=-1), n_experts - 1)
    nused = (pad_ends[-1:] // rb).astype(i32)
    nvalid = jnp.clip((pad_starts + counts)[blk_e] - blk_start[:, 0], 0, rb).astype(i32)
    return dest.astype(i32), blk_e, nused, nvalid


def kernel(x, w_in, rel_bias, shift_mu, w0, w_up, a0, a_up, g_up, k_k, k_a, r_k, lnx_w, lnx_b,
           proj_a, proj_b, w_out, ln1_w, ln1_b, w_router, b_router, w_gu, b_gu, w_down, b_down,
           ln2_w, ln2_b):
    B, S, D = x.shape
    T = B * S
    h = x.reshape(T, D)
    for l in range(DEPTH):
        E, F = w_down.shape[1], w_down.shape[2]
        n_blocks = (T * TOP_K + E * (MOE_RB - 1) + MOE_RB - 1) // MOE_RB
        proj, y_a, y_b, w_down_b, w_gu_b, xs0 = _token_mixing(
            h, B, S, w_in[l], rel_bias[l], shift_mu[l], w0[l], w_up[l], a0[l], a_up[l], g_up[l],
            k_k[l], k_a[l], r_k[l], lnx_w[l], lnx_b[l],
            w_down[l].reshape(E * F, D), w_gu[l].reshape(E * D, 2 * F), (n_blocks * MOE_RB, D))
        h1, topi, gate, rank, counts = _merge_router(
            h, y_a, y_b, proj, proj_a[l].astype(bf16), proj_b[l].astype(bf16), w_out[l].astype(bf16),
            ln1_w[l].reshape(1, D), ln1_b[l].reshape(1, D), w_router[l].T, b_router[l].reshape(E, 1))
        gate = jnp.pad(gate[:TOP_K].T, ((0, 0), (0, LANES - TOP_K)))
        dest, blk_e, nused, nvalid = _moe_routing(
            topi[:TOP_K].T, rank[:TOP_K].T, counts[:, 0], E, MOE_RB, n_blocks)
        xs = _dispatch(dest, h1, xs0)
        y = _experts(blk_e, nused, nvalid, xs, w_gu_b.reshape(E, D, 2 * F), b_gu[l].reshape(E, 1, -1),
                     w_down_b.reshape(E, F, D), b_down[l].reshape(E, 1, D))
        h = _combine(dest, h1, gate, ln2_w[l].reshape(1, D), ln2_b[l].reshape(1, D), y)
    return h.reshape(B, S, D)
```

```python
import jax
import jax.numpy as jnp
from jax import lax
from jax.experimental import pallas as pl
from jax.experimental.pallas import tpu as pltpu

f32 = jnp.float32
bf16 = jnp.bfloat16
i32 = jnp.int32

CHUNK = 64
LEFT_CHUNKS = 8
HEAD_DIM = 64
ATT_WIDTH = 1024
MAX_PAST_DIST = 256
REL_TABLE = MAX_PAST_DIST + CHUNK
RWKV_WIDTH = 1024
DECAY_LORA = 96
ICLR_LORA = 96
GATE_LORA = 256
TOP_K = 4
SWIGLU_LIMIT = 7.0
SWIGLU_ALPHA = 1.702
LN_EPS = 1e-5
GN_EPS = 64e-5
DEPTH = 1
DEEPNORM_ALPHA = (2 * DEPTH) ** 0.25

LANES = 128
PAIR = 2 * HEAD_DIM
F32_SUBLANES = 8
BF16_SUBLANES = 16
VMEM_LIMIT = 56 * 1024 * 1024

NEG = -1e30

LORA_PAD = 512
COL_GATE = 0
COL_Q = 4096
COL_K = COL_Q + ATT_WIDTH
COL_V = COL_K + ATT_WIDTH
COL_R = COL_V + ATT_WIDTH
COL_KR = COL_R + RWKV_WIDTH
COL_VR = COL_KR + RWKV_WIDTH
COL_LORA = COL_VR + RWKV_WIDTH
PROJ_COLS = COL_LORA + LORA_PAD

ATT_TQ = 256
ATT_HB = 16
RWKV_TR = 1024
LORA_TR = 1024
RWKV_SCAN_GROUP = 4
MERGE_TM = 512
MOE_RB = 512
MOE_TF = 1024
COMBINE_TM = 256
DMA_UNROLL = 4


def _dot(a, b, precision=None):
    return jnp.dot(a, b, preferred_element_type=f32, precision=precision)


def _dot_nt(a, b):
    return lax.dot_general(a, b, (((1,), (1,)), ((), ())), preferred_element_type=f32)


def _dot_tn(a, b):
    return lax.dot_general(a, b, (((0,), (0,)), ((), ())), preferred_element_type=f32)


def _split_bf16(a):
    hi = a.astype(bf16)
    return hi, (a - hi.astype(f32)).astype(bf16)


def _dot_split_lhs(a, b_bf16):
    hi, lo = _split_bf16(a)
    return _dot(hi, b_bf16) + _dot(lo, b_bf16)


def _dot_split_rhs(a_bf16, b):
    hi, lo = _split_bf16(b)
    return _dot(a_bf16, hi) + _dot(a_bf16, lo)


def _inproj_kernel(x_ref, w_ref, o_ref, xb_ref):
    @pl.when(pl.program_id(1) == 0)
    def _():
        xb_ref[...] = x_ref[...].astype(bf16)

    o_ref[...] = _dot(xb_ref[...], w_ref[...]).astype(o_ref.dtype)


def _in_proj(x2, w, tm, tn):
    M, K = x2.shape
    N = w.shape[1]
    return pl.pallas_call(
        _inproj_kernel,
        out_shape=jax.ShapeDtypeStruct((M, N), bf16),
        grid=(M // tm, N // tn),
        in_specs=[pl.BlockSpec((tm, K), lambda i, j: (i, 0)),
                  pl.BlockSpec((K, tn), lambda i, j: (0, j))],
        out_specs=pl.BlockSpec((tm, tn), lambda i, j: (i, j)),
        scratch_shapes=[pltpu.VMEM((tm, K), bf16)],
        compiler_params=pltpu.CompilerParams(
            dimension_semantics=("parallel", "arbitrary"), vmem_limit_bytes=VMEM_LIMIT),
        name="in_proj",
    )(x2, w)


def _attn_bias_table(rel_bias):
    tq = ATT_TQ
    nk = 3 * tq
    H = rel_bias.shape[0]
    C = CHUNK
    nqc, nkc = tq // C, nk // C
    e = jnp.arange(-(nkc - 1), nqc)
    dist = (e[:, None] * C + jnp.arange(-(C - 1), C)[None, :]) + 2 * tq
    idx = jnp.clip(jnp.minimum(dist, MAX_PAST_DIST) + (C - 1), 0, REL_TABLE - 1)
    w = jnp.pad(rel_bias.astype(f32)[:, idx], ((0, 0), (0, 0), (0, 1)))
    ne = e.shape[0]
    G = jnp.broadcast_to(w[:, :, None, :], (H, ne, C, 2 * C)).reshape(H, ne, C * 2 * C)
    G = G[:, :, :C * (2 * C - 1)].reshape(H, ne, C, 2 * C - 1)
    blocks = jnp.swapaxes(G[:, :, :, C - 1:], 2, 3)
    qc = jnp.arange(nqc)[:, None]
    kc = jnp.arange(nkc)[None, :]
    d = qc + (2 * tq) // C - kc
    tiles = blocks[:, (qc - kc) + (nkc - 1)]
    tiles = jnp.where(((d >= 0) & (d <= LEFT_CHUNKS))[None, :, :, None, None], tiles, NEG)
    return jnp.transpose(tiles, (0, 1, 3, 2, 4)).reshape(H, tq, nk)


def _attn_kernel(q_ref, k0_ref, k1_ref, k2_ref, v0_ref, v1_ref, v2_ref, bias_ref, side_ref,
                 o_ref, side_o_ref):
    qt = pl.program_id(2)
    side_o_ref[...] = side_ref[...].astype(side_o_ref.dtype)
    tq = q_ref.shape[0]
    k_refs = (k0_ref, k1_ref, k2_ref)
    v_refs = (v0_ref, v1_ref, v2_ref)
    heads = range(ATT_HB)
    blocks = range(3)
    lane = lax.broadcasted_iota(i32, (1, PAIR), 1)
    head0 = lane < HEAD_DIM
    pair = [slice((h // 2) * PAIR, (h // 2 + 1) * PAIR) for h in heads]
    q = [jnp.where(head0 if h % 2 == 0 else jnp.logical_not(head0),
                   q_ref[:, pair[h]] * (HEAD_DIM ** -0.5), 0.0).astype(bf16) for h in heads]
    s = [[_dot_nt(q[h], k_refs[j][:, pair[h]]) + bias_ref[h, :, j * tq:(j + 1) * tq] for j in blocks]
         for h in heads]
    s = [[jnp.where(qt - 2 + j >= 0, s[h][j], NEG) if j < 2 else s[h][j] for j in blocks] for h in heads]
    m = [jnp.max(jnp.maximum(jnp.maximum(s[h][0], s[h][1]), s[h][2]), axis=-1, keepdims=True)
         for h in heads]
    p = [[jnp.exp((s[h][j] - m[h]).astype(bf16)) for j in blocks] for h in heads]
    ones = jnp.ones((tq, PAIR), bf16)
    v1 = [[jnp.concatenate([v_refs[j][:, pair[2 * g]], ones], axis=1) for j in blocks]
          for g in range(ATT_HB // 2)]
    acc = [_dot(p[h][0], v1[h // 2][0]) + _dot(p[h][1], v1[h // 2][1]) + _dot(p[h][2], v1[h // 2][2])
           for h in heads]
    out = [acc[h][:, 0:PAIR] / acc[h][:, PAIR:PAIR + 1] for h in heads]
    for g in range(ATT_HB // 2):
        o_ref[:, pair[2 * g]] = jnp.where(head0, out[2 * g], out[2 * g + 1]).astype(o_ref.dtype)


def _attention(proj, bias, B, S, side):
    T = B * S
    tq = ATT_TQ
    nt = S // tq
    width = ATT_HB * HEAD_DIM
    ngrp = ATT_WIDTH // width
    side_spec = pl.BlockSpec((side.shape[0] // (B * nt * ngrp), side.shape[1]),
                             lambda h, b, t: ((h * B + b) * nt + t, 0))
    qcol, kcol, vcol = COL_Q // width, COL_K // width, COL_V // width

    def kv_spec(col, back):
        return pl.BlockSpec((tq, width),
                            lambda h, b, t: (b * nt + jnp.maximum(t - back, 0), col + h))

    return pl.pallas_call(
        _attn_kernel,
        out_shape=(jax.ShapeDtypeStruct((T, ATT_WIDTH), bf16), jax.ShapeDtypeStruct(side.shape, bf16)),
        grid=(ngrp, B, nt),
        in_specs=[pl.BlockSpec((tq, width), lambda h, b, t: (b * nt + t, qcol + h)),
                  kv_spec(kcol, 2), kv_spec(kcol, 1), kv_spec(kcol, 0),
                  kv_spec(vcol, 2), kv_spec(vcol, 1), kv_spec(vcol, 0),
                  pl.BlockSpec((ATT_HB, tq, 3 * tq), lambda h, b, t: (h, 0, 0)),
                  side_spec],
        out_specs=(pl.BlockSpec((tq, width), lambda h, b, t: (b * nt + t, h)), side_spec),
        compiler_params=pltpu.CompilerParams(
            dimension_semantics=("parallel", "parallel", "parallel"), vmem_limit_bytes=VMEM_LIMIT),
        name="chunk_attn",
    )(proj, proj, proj, proj, proj, proj, proj, bias, side)


def _lora_prep_kernel(lo_ref, lop_ref, mu_ref, o_ref):
    t = pl.program_id(1)
    tr = lo_ref.shape[0]
    p = lo_ref[...].astype(f32)
    prev_row = lop_ref[BF16_SUBLANES - 1:BF16_SUBLANES, :].astype(f32) * jnp.where(t == 0, 0.0, 1.0)
    row = lax.broadcasted_iota(i32, (tr, 1), 0)
    prev = jnp.where(row == 0, prev_row, pltpu.roll(p, 1, 0))
    lo = p + (prev - p) * mu_ref[...]
    o_ref[:, 0:LANES] = jnp.tanh(lo[:, 0:LANES]).astype(o_ref.dtype)
    o_ref[:, LANES:2 * LANES] = lo[:, LANES:2 * LANES].astype(o_ref.dtype)
    o_ref[:, 2 * LANES:] = jax.nn.sigmoid(lo[:, 2 * LANES:]).astype(o_ref.dtype)


def _lora_prep(proj, B, S, mu_lora):
    tr = LORA_TR
    nt = S // tr
    sub = tr // BF16_SUBLANES
    nsub = S // BF16_SUBLANES
    locol = COL_LORA // LORA_PAD
    return pl.pallas_call(
        _lora_prep_kernel,
        out_shape=jax.ShapeDtypeStruct((B * S, LORA_PAD), bf16),
        grid=(B, nt),
        in_specs=[pl.BlockSpec((tr, LORA_PAD), lambda b, t: (b * nt + t, locol)),
                  pl.BlockSpec((BF16_SUBLANES, LORA_PAD),
                               lambda b, t: (b * nsub + jnp.maximum(t * sub - 1, 0), locol)),
                  pl.BlockSpec((1, LORA_PAD), lambda b, t: (0, 0))],
        out_specs=pl.BlockSpec((tr, LORA_PAD), lambda b, t: (b * nt + t, 0)),
        compiler_params=pltpu.CompilerParams(
            dimension_semantics=("parallel", "parallel"), vmem_limit_bytes=VMEM_LIMIT),
        name="rwkv7_lora_prep",
    )(proj, proj, mu_lora)


def _rwkv_kernel(r_ref, k_ref, v_ref, act_ref, rp_ref, kp_ref, vp_ref,
                 mur_ref, muk_ref, muv_ref, w0_ref, a0_ref, kkp_ref, kap_ref, rkp_ref,
                 lnw_ref, lnb_ref, wup_ref, aup_ref, gup_ref, side_ref,
                 o_ref, side_o_ref, zero_o_ref,
                 st_ref, r_s, lw_s, k_s, v_s, a_s, b_s, g_s, bon_s):
    t = pl.program_id(2)
    tr = r_ref.shape[0]
    C = CHUNK

    side_o_ref[...] = side_ref[...].astype(side_o_ref.dtype)
    zero_o_ref[...] = jnp.zeros_like(zero_o_ref)

    @pl.when(t == 0)
    def _():
        st_ref[...] = jnp.zeros_like(st_ref)

    not_first = jnp.where(t == 0, 0.0, 1.0).astype(f32)
    row = lax.broadcasted_iota(i32, (tr, 1), 0)

    def shift(p_ref, prev_ref, mu_ref):
        p = p_ref[...].astype(f32)
        prev_row = prev_ref[BF16_SUBLANES - 1:BF16_SUBLANES, :].astype(f32) * not_first
        prev = jnp.where(row == 0, prev_row, pltpu.roll(p, 1, 0))
        return p + (prev - p) * mu_ref[...]

    r = shift(r_ref, rp_ref, mur_ref)
    k = shift(k_ref, kp_ref, muk_ref)
    v = shift(v_ref, vp_ref, muv_ref)

    ri = lax.broadcasted_iota(i32, (PAIR, PAIR), 0)
    ci = lax.broadcasted_iota(i32, (PAIR, PAIR), 1)
    same_head = (ri // HEAD_DIM) == (ci // HEAD_DIM)
    head_ones = jnp.where(same_head, 1.0, 0.0).astype(bf16)
    head_avg = jnp.where(same_head, 1.0 / HEAD_DIM, 0.0).astype(bf16)

    z = w0_ref[...] + _dot(act_ref[:, 0:LANES], wup_ref[...])
    softplus_negz = jnp.maximum(-z, 0.0) + jnp.log(1.0 + jnp.exp(-jnp.abs(z)))
    lw = -jnp.exp(-softplus_negz - 0.5)
    alr = jax.nn.sigmoid(a0_ref[...] + _dot(act_ref[:, LANES:2 * LANES], aup_ref[...]))
    g = _dot(act_ref[:, 2 * LANES:], gup_ref[...])
    kk = k * kkp_ref[...]
    kk = kk / jnp.maximum(jnp.sqrt(_dot_split_lhs(kk * kk, head_ones)), 1e-12)
    k2 = k * (1.0 + (alr - 1.0) * kap_ref[...])
    bonus = _dot_split_lhs(r * k2 * rkp_ref[...], head_ones) * v

    r_s[...] = r
    lw_s[...] = lw
    k_s[...] = k2
    v_s[...] = v
    a_s[...] = -kk
    b_s[...] = kk * alr
    g_s[...] = g
    bon_s[...] = bonus

    lane = lax.broadcasted_iota(i32, (1, PAIR), 1)
    head0 = lane < HEAD_DIM
    tri_incl = jnp.where(lax.broadcasted_iota(i32, (C, C), 1) <= lax.broadcasted_iota(i32, (C, C), 0),
                         1.0, 0.0).astype(bf16)
    strict = ci < ri
    incl = ci <= ri
    eye = ci == ri
    lnw = lnw_ref[...]
    lnb = lnb_ref[...]

    def stack(x):
        return jnp.concatenate([jnp.where(head0, x, 0.0), jnp.where(head0, 0.0, x)], axis=0)

    nc = tr // C
    chunks = range(nc)
    rows = [slice(c * C, (c + 1) * C) for c in chunks]
    lwc = [lw_s[rw, :] for rw in rows]
    L = [_dot_split_rhs(tri_incl, x) for x in lwc]
    LC = [x[C - 1:C, :] for x in L]
    a_t, r_t, r_tb, bk_t, v_st = [], [], [], [], []
    for c in chunks:
        rw = rows[c]
        e_neg = jnp.exp(-L[c])
        a_t.append(stack(a_s[rw, :] * jnp.exp(L[c] - lwc[c])).astype(bf16))
        r_t.append(stack(r_s[rw, :] * jnp.exp(L[c])))
        r_tb.append(r_t[c].astype(bf16))
        bk_t.append(jnp.concatenate([stack(b_s[rw, :] * e_neg), stack(k_s[rw, :] * e_neg)], axis=0).astype(bf16))
        v_st.append(stack(v_s[rw, :]).astype(bf16))

    A = [_dot_nt(jnp.concatenate([a_t[c], r_tb[c]], axis=0), bk_t[c]) for c in chunks]
    a_ab = [jnp.where(strict, x[0:PAIR, 0:PAIR], 0.0) for x in A]
    a_kk = [jnp.concatenate([jnp.where(strict, x[0:PAIR, PAIR:], 0.0),
                             jnp.where(incl, x[PAIR:, PAIR:], 0.0)], axis=0).astype(bf16) for x in A]
    a_rb = [jnp.where(incl, x[PAIR:, 0:PAIR], 0.0).astype(bf16) for x in A]

    def sub_diag(s):
        return jnp.logical_and(jnp.logical_and((ri // (2 * s)) == (ci // (2 * s)), (ri // s) % 2 == 1),
                               (ci // s) % 2 == 0)

    tinv = [jnp.where(eye, 1.0, 0.0).astype(f32) + jnp.where(sub_diag(1), x, 0.0) for x in a_ab]
    s_blk = 2
    while s_blk < HEAD_DIM:
        mask = sub_diag(s_blk)
        n21 = [jnp.where(mask, x, 0.0).astype(bf16) for x in a_ab]
        tb = [tv.astype(bf16) for tv in tinv]
        n21_t11 = [_dot(x, tv).astype(bf16) for x, tv in zip(n21, tb)]
        tinv = [tv + _dot(tvb, x) for tv, tvb, x in zip(tinv, tb, n21_t11)]
        s_blk *= 2

    av = [_dot(a_kk[c], v_st[c]) for c in chunks]
    wu = [_dot(tinv[c].astype(bf16), jnp.concatenate([a_t[c], av[c][0:PAIR].astype(bf16)], axis=1))
          for c in chunks]
    wub = [x.astype(bf16) for x in wu]
    ry = [jnp.concatenate([r_t[c], av[c][PAIR:]], axis=1) + _dot(a_rb[c], wub[c]) for c in chunks]
    ry = [x[0:C] + x[C:] for x in ry]

    mc, gc = [], []
    for c in chunks:
        rw = rows[c]
        e_end = jnp.exp(LC[c] - L[c])
        b_h = stack(b_s[rw, :] * e_end).astype(bf16)
        k_h = stack(k_s[rw, :] * e_end).astype(bf16)
        mc.append(jnp.where(eye, jnp.exp(LC[c]), 0.0) + _dot_tn(wub[c][:, 0:PAIR], b_h))
        gc.append(_dot_tn(jnp.concatenate([wub[c][:, PAIR:], v_st[c]], axis=0),
                          jnp.concatenate([b_h, k_h], axis=0)))

    mcb = [x.astype(bf16) for x in mc]
    gs = RWKV_SCAN_GROUP
    ng = nc // gs
    pq = [jnp.concatenate([mc[g * gs], gc[g * gs]], axis=0) for g in range(ng - 1)]
    for j in range(1, gs):
        pq = [_dot(pq[g].astype(bf16), mcb[g * gs + j])
              + jnp.concatenate([jnp.zeros((PAIR, PAIR), f32), gc[g * gs + j]], axis=0) for g in range(ng - 1)]
    entry = [st_ref[...]]
    for g in range(ng - 1):
        entry.append(_dot(entry[g].astype(bf16), pq[g][0:PAIR].astype(bf16)) + pq[g][PAIR:])
    y = [None] * nc
    cur = entry
    for j in range(gs):
        curb = [x.astype(bf16) for x in cur]
        for g in range(ng):
            c = g * gs + j
            y[c] = _dot_nt(ry[c][:, 0:PAIR].astype(bf16), curb[g]) + ry[c][:, PAIR:]
        if j + 1 < gs:
            cur = [_dot(curb[g], mcb[g * gs + j]) + gc[g * gs + j] for g in range(ng)]
        else:
            st_ref[...] = _dot(curb[ng - 1], mcb[nc - 1]) + gc[nc - 1]

    mu = [_dot(x.astype(bf16), head_avg) for x in y]
    d = [x - m for x, m in zip(y, mu)]
    var = [_dot((x * x).astype(bf16), head_avg) for x in d]
    for c in chunks:
        yn = d[c] * lax.rsqrt(var[c] + GN_EPS) * lnw + lnb
        o_ref[rows[c], :] = ((yn + bon_s[rows[c], :]) * g_s[rows[c], :]).astype(o_ref.dtype)


def _rwkv(proj, act, B, S, mu_rkv, w0, a0, k_k, k_a, r_k, lnx_w, lnx_b, w_up_p, a_up_p, g_up, side,
          zero_shape):
    T = B * S
    tr = RWKV_TR
    nt = S // tr
    sub = tr // BF16_SUBLANES
    nsub = S // BF16_SUBLANES
    npair = RWKV_WIDTH // PAIR
    rcol, kcol, vcol = COL_R // PAIR, COL_KR // PAIR, COL_VR // PAIR
    side_rows = side.shape[0] // (B * npair * nt)
    zero_rows = zero_shape[0] // (B * npair * nt)
    step = lambda b, h, t: ((b * npair + h) * nt + t, 0)

    def cur(col):
        return pl.BlockSpec((tr, PAIR), lambda b, h, t: (b * nt + t, col + h))

    def prev(col):
        return pl.BlockSpec((BF16_SUBLANES, PAIR),
                            lambda b, h, t: (b * nsub + jnp.maximum(t * sub - 1, 0), col + h))

    def vec(off):
        return pl.BlockSpec((1, PAIR), lambda b, h, t: (0, off + h))

    in_specs = [
        cur(rcol), cur(kcol), cur(vcol),
        pl.BlockSpec((tr, LORA_PAD), lambda b, h, t: (b * nt + t, 0)),
        prev(rcol), prev(kcol), prev(vcol),
        vec(0), vec(npair), vec(2 * npair),
        vec(0), vec(0), vec(0), vec(0), vec(0), vec(0), vec(0),
        pl.BlockSpec((LANES, PAIR), lambda b, h, t: (0, h)),
        pl.BlockSpec((LANES, PAIR), lambda b, h, t: (0, h)),
        pl.BlockSpec((GATE_LORA, PAIR), lambda b, h, t: (0, h)),
        pl.BlockSpec((side_rows, side.shape[1]), step),
    ]
    scratch = [pltpu.VMEM((PAIR, PAIR), f32)] + [pltpu.VMEM((tr, PAIR), f32)] * 8
    return pl.pallas_call(
        _rwkv_kernel,
        out_shape=(jax.ShapeDtypeStruct((T, RWKV_WIDTH), bf16),
                   jax.ShapeDtypeStruct(side.shape, bf16),
                   jax.ShapeDtypeStruct(zero_shape, f32)),
        grid=(B, npair, nt),
        in_specs=in_specs,
        out_specs=(pl.BlockSpec((tr, PAIR), lambda b, h, t: (b * nt + t, h)),
                   pl.BlockSpec((side_rows, side.shape[1]), step),
                   pl.BlockSpec((zero_rows, zero_shape[1]), step)),
        scratch_shapes=scratch,
        compiler_params=pltpu.CompilerParams(
            dimension_semantics=("parallel", "parallel", "arbitrary"), vmem_limit_bytes=VMEM_LIMIT),
        name="rwkv7",
    )(proj, proj, proj, act, proj, proj, proj,
      mu_rkv, mu_rkv, mu_rkv, w0, a0, k_k, k_a, r_k, lnx_w, lnx_b, w_up_p, a_up_p, g_up, side)


def _layer_norm(z, w, b):
    mu = jnp.mean(z, axis=-1, keepdims=True)
    d = z - mu
    var = jnp.mean(d * d, axis=-1, keepdims=True)
    return d * lax.rsqrt(var + LN_EPS) * w + b


def _merge_kernel(x_ref, ya_ref, yb_ref, ga_ref, gb_ref, pa_ref, pb_ref, wo_ref, l1w_ref, l1b_ref,
                  wrt_ref, brt_ref,
                  h_ref, topi_ref, gate_ref, rank_ref, cnt_ref,
                  run_ref):
    i = pl.program_id(0)
    tm = x_ref.shape[0]
    ne = wrt_ref.shape[0]

    @pl.when(i == 0)
    def _():
        run_ref[...] = jnp.zeros_like(run_ref)

    halves = [slice(0, tm // 2), slice(tm // 2, tm)]
    ma = [_dot(ya_ref[rw, :], pa_ref[...]) for rw in halves]
    mb = [_dot(yb_ref[rw, :], pb_ref[...]) for rw in halves]
    merged = [(jax.nn.sigmoid(ga_ref[rw, :].astype(f32)) * a
               + jax.nn.sigmoid(gb_ref[rw, :].astype(f32)) * b).astype(bf16)
              for rw, a, b in zip(halves, ma, mb)]
    mix = [_dot(m, wo_ref[...]) for m in merged]
    h = jnp.concatenate([_layer_norm(DEEPNORM_ALPHA * x_ref[rw, :] + mx, l1w_ref[...], l1b_ref[...])
                         for rw, mx in zip(halves, mix)], axis=0)
    h_ref[...] = h

    h_hi, h_lo = _split_bf16(h)
    w_hi, w_lo = _split_bf16(wrt_ref[...])
    prod = _dot_nt(jnp.concatenate([w_hi, w_lo], axis=0), jnp.concatenate([h_hi, h_lo], axis=0))
    lg = prod[0:ne, 0:tm] + (prod[0:ne, tm:] + prod[ne:, 0:tm]) + brt_ref[...]
    eidx = lax.broadcasted_iota(i32, (ne, tm), 0)
    slot = lax.broadcasted_iota(i32, (F32_SUBLANES, tm), 0)
    vals, idxs = [], []
    for _ in range(TOP_K):
        m = jnp.max(lg, axis=0, keepdims=True)
        idx = jnp.min(jnp.where(lg == m, eidx, ne), axis=0, keepdims=True)
        vals.append(m)
        idxs.append(idx)
        lg = jnp.where(eidx == idx, -jnp.inf, lg)
    exps = [jnp.exp(vk - vals[0]) for vk in vals]
    denom = exps[0] + exps[1] + exps[2] + exps[3]

    onehots = [jnp.where(eidx == idx, 1.0, 0.0).astype(f32) for idx in idxs]
    oh = onehots[0] + onehots[1] + onehots[2] + onehots[3]
    tri = jnp.where(lax.broadcasted_iota(i32, (tm, tm), 0) < lax.broadcasted_iota(i32, (tm, tm), 1),
                    1.0, 0.0).astype(bf16)
    before = run_ref[...] + _dot(oh.astype(bf16), tri)

    topi_o = jnp.zeros((F32_SUBLANES, tm), i32)
    gate_o = jnp.zeros((F32_SUBLANES, tm), f32)
    rank_o = jnp.zeros((F32_SUBLANES, tm), i32)
    for kslot in range(TOP_K):
        rk = jnp.sum(onehots[kslot] * before, axis=0, keepdims=True).astype(i32)
        topi_o = jnp.where(slot == kslot, idxs[kslot], topi_o)
        gate_o = jnp.where(slot == kslot, exps[kslot] / denom, gate_o)
        rank_o = jnp.where(slot == kslot, rk, rank_o)
    topi_ref[...] = topi_o
    gate_ref[...] = gate_o
    rank_ref[...] = rank_o
    run = run_ref[...] + jnp.sum(oh, axis=1, keepdims=True)
    run_ref[...] = run
    cnt_ref[...] = run


def _merge_router(x2, ya, yb, proj, pa, pb, wo, l1w, l1b, wrt, brt):
    T, D = x2.shape
    tm = MERGE_TM
    ne = wrt.shape[0]
    gcol = COL_GATE // D

    def const(shape):
        return pl.BlockSpec(shape, lambda i: (0, 0), pipeline_mode=pl.Buffered(1))

    out_shape = (jax.ShapeDtypeStruct((T, D), f32),
                 jax.ShapeDtypeStruct((F32_SUBLANES, T), i32),
                 jax.ShapeDtypeStruct((F32_SUBLANES, T), f32),
                 jax.ShapeDtypeStruct((F32_SUBLANES, T), i32),
                 jax.ShapeDtypeStruct((ne, 1), f32))
    return pl.pallas_call(
        _merge_kernel,
        out_shape=out_shape,
        grid=(T // tm,),
        in_specs=[pl.BlockSpec((tm, D), lambda i: (i, 0)),
                  pl.BlockSpec((tm, ATT_WIDTH), lambda i: (i, 0)),
                  pl.BlockSpec((tm, RWKV_WIDTH), lambda i: (i, 0)),
                  pl.BlockSpec((tm, D), lambda i: (i, gcol)),
                  pl.BlockSpec((tm, D), lambda i: (i, gcol + 1)),
                  const(pa.shape), const(pb.shape), const(wo.shape),
                  const((1, D)), const((1, D)), const(wrt.shape), const((ne, 1))],
        out_specs=(pl.BlockSpec((tm, D), lambda i: (i, 0)),
                   pl.BlockSpec((F32_SUBLANES, tm), lambda i: (0, i)),
                   pl.BlockSpec((F32_SUBLANES, tm), lambda i: (0, i)),
                   pl.BlockSpec((F32_SUBLANES, tm), lambda i: (0, i)),
                   pl.BlockSpec((ne, 1), lambda i: (0, 0))),
        scratch_shapes=[pltpu.VMEM((ne, 1), f32)],
        compiler_params=pltpu.CompilerParams(
            dimension_semantics=("arbitrary",), vmem_limit_bytes=VMEM_LIMIT),
        name="merge_ln1_router",
    )(x2, ya, yb, proj, proj, pa, pb, wo, l1w, l1b, wrt, brt)


def _dispatch_kernel(dest_ref, h_ref, xs_in_hbm, xs_hbm, sem):
    del xs_in_hbm
    tm = h_ref.shape[0]

    def row_copy(t, k):
        return pltpu.make_async_copy(h_ref.at[pl.ds(t, 1)],
                                     xs_hbm.at[pl.ds(dest_ref[t * TOP_K + k], 1)], sem)

    def start(t, c):
        for k in range(TOP_K):
            row_copy(t, k).start()
        return c

    def wait(t, c):
        for k in range(TOP_K):
            row_copy(t, k).wait()
        return c

    lax.fori_loop(0, tm, start, 0, unroll=DMA_UNROLL)
    lax.fori_loop(0, tm, wait, 0, unroll=DMA_UNROLL)


def _dispatch(dest_flat, h, xs0):
    T, D = h.shape
    n = COMBINE_TM * TOP_K
    return pl.pallas_call(
        _dispatch_kernel,
        out_shape=jax.ShapeDtypeStruct(xs0.shape, xs0.dtype),
        grid=(T * TOP_K // n,),
        in_specs=[pl.BlockSpec((n,), lambda i: (i,), memory_space=pltpu.SMEM),
                  pl.BlockSpec((n // TOP_K, D), lambda i: (i, 0)),
                  pl.BlockSpec(memory_space=pl.ANY)],
        out_specs=pl.BlockSpec(memory_space=pl.ANY),
        scratch_shapes=[pltpu.SemaphoreType.DMA(())],
        input_output_aliases={2: 0},
        compiler_params=pltpu.CompilerParams(
            dimension_semantics=("arbitrary",), has_side_effects=True, vmem_limit_bytes=VMEM_LIMIT),
        name="moe_dispatch",
    )(dest_flat, h, xs0)


def _expert_kernel(be_ref, nu_ref, nv_ref, x_ref, wg_ref, wu_ref, bg_ref, bu_ref, wd_ref, bd_ref, o_ref, xb_ref):
    del be_ref
    i = pl.program_id(0)
    j = pl.program_id(1)
    used = i < nu_ref[0]
    half = o_ref.shape[0] // 2
    top_only = nv_ref[i] <= half

    @pl.when(jnp.logical_and(used, j == 0))
    def _():
        xb_ref[...] = x_ref[...].astype(bf16)

    @pl.when(jnp.logical_and(jnp.logical_not(used), j == 0))
    def _():
        o_ref[...] = jnp.zeros_like(o_ref)

    def ffn(rows, first):
        xb = xb_ref[rows, :]
        g = _dot(xb, wg_ref[0]) + bg_ref[0]
        u = _dot(xb, wu_ref[0]) + bu_ref[0]
        g = jnp.minimum(g, SWIGLU_LIMIT)
        u = jnp.clip(u, -SWIGLU_LIMIT, SWIGLU_LIMIT)
        act = (u + 1.0) * (g * jax.nn.sigmoid(SWIGLU_ALPHA * g))
        y = _dot(act.astype(bf16), wd_ref[0])
        if first:
            o_ref[rows, :] = y + bd_ref[0]
        else:
            o_ref[rows, :] += y

    for first in (True, False):
        at_step = (j == 0) if first else (j > 0)

        @pl.when(jnp.logical_and(jnp.logical_and(used, at_step), jnp.logical_not(top_only)))
        def _():
            ffn(slice(None), first)

        @pl.when(jnp.logical_and(jnp.logical_and(used, at_step), top_only))
        def _():
            ffn(slice(0, half), first)
            if first:
                o_ref[half:, :] = jnp.zeros((half, o_ref.shape[1]), o_ref.dtype)


def _experts(blk_e, nused, nvalid, xs, w_gu, b_gu, w_down, b_down):
    R, D = xs.shape
    E, _, F2 = w_gu.shape
    F = F2 // 2
    rb, tf = MOE_RB, min(MOE_TF, F)
    nf = F // tf

    def jj(i, j, nu):
        return jnp.where(i < nu[0], j, 0)

    grid_spec = pltpu.PrefetchScalarGridSpec(
        num_scalar_prefetch=3,
        grid=(R // rb, nf),
        in_specs=[pl.BlockSpec((rb, D), lambda i, j, be, nu, nv: (jnp.where(i < nu[0], i, 0), 0)),
                  pl.BlockSpec((1, D, tf), lambda i, j, be, nu, nv: (be[i], 0, jj(i, j, nu))),
                  pl.BlockSpec((1, D, tf), lambda i, j, be, nu, nv: (be[i], 0, nf + jj(i, j, nu))),
                  pl.BlockSpec((1, 1, tf), lambda i, j, be, nu, nv: (be[i], 0, jj(i, j, nu))),
                  pl.BlockSpec((1, 1, tf), lambda i, j, be, nu, nv: (be[i], 0, nf + jj(i, j, nu))),
                  pl.BlockSpec((1, tf, D), lambda i, j, be, nu, nv: (be[i], jj(i, j, nu), 0)),
                  pl.BlockSpec((1, 1, D), lambda i, j, be, nu, nv: (be[i], 0, 0))],
        out_specs=pl.BlockSpec((rb, D), lambda i, j, be, nu, nv: (i, 0)),
        scratch_shapes=[pltpu.VMEM((rb, D), bf16)],
    )
    return pl.pallas_call(
        _expert_kernel,
        out_shape=jax.ShapeDtypeStruct((R, D), f32),
        grid_spec=grid_spec,
        compiler_params=pltpu.CompilerParams(
            dimension_semantics=("parallel", "arbitrary"), vmem_limit_bytes=VMEM_LIMIT),
        name="moe_experts",
    )(blk_e, nused, nvalid, xs, w_gu, w_gu, b_gu, b_gu, w_down, b_down)


def _combine_kernel(dest_ref, dest_next_ref, h_ref, gate_ref, l2w_ref, l2b_ref, y_hbm, o_ref, buf, sem):
    i = pl.program_id(0)
    tm = h_ref.shape[0]
    slot = i % 2

    def row_copy(d_ref, sl, t, k):
        return pltpu.make_async_copy(y_hbm.at[pl.ds(d_ref[t * TOP_K + k], 1)],
                                     buf.at[sl, k, pl.ds(t, 1)], sem.at[sl])

    def start_tile(d_ref, sl):
        def body(t, c):
            for k in range(TOP_K):
                row_copy(d_ref, sl, t, k).start()
            return c
        lax.fori_loop(0, tm, body, 0, unroll=DMA_UNROLL)

    def wait_tile(d_ref, sl):
        def body(t, c):
            for k in range(TOP_K):
                row_copy(d_ref, sl, t, k).wait()
            return c
        lax.fori_loop(0, tm, body, 0, unroll=DMA_UNROLL)

    @pl.when(i == 0)
    def _():
        start_tile(dest_ref, 0)

    @pl.when(i + 1 < pl.num_programs(0))
    def _():
        start_tile(dest_next_ref, 1 - slot)

    wait_tile(dest_ref, slot)

    gate = gate_ref[...]
    ffn = gate[:, 0:1] * buf[slot, 0]
    for kslot in range(1, TOP_K):
        ffn = ffn + gate[:, kslot:kslot + 1] * buf[slot, kslot]
    o_ref[...] = _layer_norm(DEEPNORM_ALPHA * h_ref[...] + ffn, l2w_ref[...], l2b_ref[...])


def _combine(dest_flat, h, gate, l2w, l2b, y):
    T, D = h.shape
    tm = COMBINE_TM
    n = tm * TOP_K
    steps = T // tm
    return pl.pallas_call(
        _combine_kernel,
        out_shape=jax.ShapeDtypeStruct((T, D), f32),
        grid=(steps,),
        in_specs=[pl.BlockSpec((n,), lambda i: (i,), memory_space=pltpu.SMEM),
                  pl.BlockSpec((n,), lambda i: (jnp.minimum(i + 1, steps - 1),), memory_space=pltpu.SMEM),
                  pl.BlockSpec((tm, D), lambda i: (i, 0)),
                  pl.BlockSpec((tm, LANES), lambda i: (i, 0)),
                  pl.BlockSpec((1, D), lambda i: (0, 0)),
                  pl.BlockSpec((1, D), lambda i: (0, 0)),
                  pl.BlockSpec(memory_space=pl.ANY)],
        out_specs=pl.BlockSpec((tm, D), lambda i: (i, 0)),
        scratch_shapes=[pltpu.VMEM((2, TOP_K, tm, D), f32), pltpu.SemaphoreType.DMA((2,))],
        compiler_params=pltpu.CompilerParams(
            dimension_semantics=("arbitrary",), vmem_limit_bytes=VMEM_LIMIT),
        name="moe_combine_ln2",
    )(dest_flat, dest_flat, h, gate, l2w, l2b, y)


def _pack_w_in(w_in):
    o1 = 3 * ATT_WIDTH
    o2 = o1 + 3 * RWKV_WIDTH
    o3 = o2 + DECAY_LORA
    o4 = o3 + ICLR_LORA
    o5 = o4 + GATE_LORA
    padc = lambda w, n: jnp.pad(w, ((0, 0), (0, n - w.shape[1])))
    parts = [w_in[:, o5:], w_in[:, :o2], padc(w_in[:, o2:o3], LANES), padc(w_in[:, o3:o4], LANES),
             w_in[:, o4:o5]]
    return jnp.concatenate(parts, axis=1).astype(bf16)


def _token_mixing(x2, B, S, w_in, rel_bias, shift_mu, w0, w_up, a0, a_up, g_up, k_k, k_a, r_k,
                  lnx_w, lnx_b, side_a, side_b, zero_shape):
    T, D = x2.shape
    wp = _pack_w_in(w_in)
    proj = _in_proj(x2, wp, min(1024, T), 1536)

    y_a, side_a = _attention(proj, _attn_bias_table(rel_bias), B, S, side_a)

    W = RWKV_WIDTH
    row = lambda p: p.reshape(1, -1).astype(f32)
    padv = lambda p, n: jnp.pad(p, (0, n - p.shape[0]))
    mu_rkv = row(shift_mu[:3 * W])
    mu_lora = row(jnp.concatenate([padv(shift_mu[3 * W:3 * W + DECAY_LORA], LANES),
                                   padv(shift_mu[3 * W + DECAY_LORA:3 * W + DECAY_LORA + ICLR_LORA], LANES),
                                   shift_mu[3 * W + DECAY_LORA + ICLR_LORA:]]))
    padr = lambda w: jnp.pad(w, ((0, LANES - w.shape[0]), (0, 0))).astype(bf16)
    act = _lora_prep(proj, B, S, mu_lora)
    y_b, side_b, zeros = _rwkv(proj, act, B, S, mu_rkv, row(w0), row(a0), row(k_k), row(k_a), row(r_k),
                               row(lnx_w), row(lnx_b), padr(w_up), padr(a_up), g_up.astype(bf16), side_b,
                               zero_shape)
    return proj, y_a, y_b, side_a, side_b, zeros


def _moe_routing(topi, rank, counts, n_experts, rb, n_blocks):
    M = topi.shape[0] * TOP_K
    counts = counts.astype(i32)
    padded = (counts + rb - 1) // rb * rb
    pad_ends = jnp.cumsum(padded)
    pad_starts = pad_ends - padded
    onehot = topi[..., None] == jnp.arange(n_experts, dtype=i32)
    dest = (jnp.sum(jnp.where(onehot, pad_starts, 0), axis=-1) + rank).reshape(M)
    blk_start = jnp.arange(n_blocks, dtype=i32)[:, None] * rb
    blk_e = jnp.minimum(jnp.sum((pad_ends[None, :] <= blk_start).astype(i32), axis=-1), n_experts - 1)
    nused = (pad_ends[-1:] // rb).astype(i32)
    nvalid = jnp.clip((pad_starts + counts)[blk_e] - blk_start[:, 0], 0, rb).astype(i32)
    return dest.astype(i32), blk_e, nused, nvalid


def kernel(x, w_in, rel_bias, shift_mu, w0, w_up, a0, a_up, g_up, k_k, k_a, r_k, lnx_w, lnx_b,
           proj_a, proj_b, w_out, ln1_w, ln1_b, w_router, b_router, w_gu, b_gu, w_down, b_down,
           ln2_w, ln2_b):
    B, S, D = x.shape
    T = B * S
    h = x.reshape(T, D)
    for l in range(DEPTH):
        E, F = w_down.shape[1], w_down.shape[2]
        n_blocks = (T * TOP_K + E * (MOE_RB - 1) + MOE_RB - 1) // MOE_RB
        proj, y_a, y_b, w_down_b, w_gu_b, xs0 = _token_mixing(
            h, B, S, w_in[l], rel_bias[l], shift_mu[l], w0[l], w_up[l], a0[l], a_up[l], g_up[l],
            k_k[l], k_a[l], r_k[l], lnx_w[l], lnx_b[l],
            w_down[l].reshape(E * F, D), w_gu[l].reshape(E * D, 2 * F), (n_blocks * MOE_RB, D))
        h1, topi, gate, rank, counts = _merge_router(
            h, y_a, y_b, proj, proj_a[l].astype(bf16), proj_b[l].astype(bf16), w_out[l].astype(bf16),
            ln1_w[l].reshape(1, D), ln1_b[l].reshape(1, D), w_router[l].T, b_router[l].reshape(E, 1))
        gate = jnp.pad(gate[:TOP_K].T, ((0, 0), (0, LANES - TOP_K)))
        dest, blk_e, nused, nvalid = _moe_routing(
            topi[:TOP_K].T, rank[:TOP_K].T, counts[:, 0], E, MOE_RB, n_blocks)
        xs = _dispatch(dest, h1, xs0)
        y = _experts(blk_e, nused, nvalid, xs, w_gu_b.reshape(E, D, 2 * F), b_gu[l].reshape(E, 1, -1),
                     w_down_b.reshape(E, F, D), b_down[l].reshape(E, 1, D))
        h = _combine(dest, h1, gate, ln2_w[l].reshape(1, D), ln2_b[l].reshape(1, D), y)
    return h.reshape(B, S, D)
```

```python
import jax
import jax.numpy as jnp
from jax import lax
from jax.experimental import pallas as pl
from jax.experimental.pallas import tpu as pltpu

f32 = jnp.float32
bf16 = jnp.bfloat16
i32 = jnp.int32

CHUNK = 64
LEFT_CHUNKS = 8
HEAD_DIM = 64
ATT_WIDTH = 1024
MAX_PAST_DIST = 256
REL_TABLE = MAX_PAST_DIST + CHUNK
RWKV_WIDTH = 1024
DECAY_LORA = 96
ICLR_LORA = 96
GATE_LORA = 256
TOP_K = 4
SWIGLU_LIMIT = 7.0
SWIGLU_ALPHA = 1.702
LN_EPS = 1e-5
GN_EPS = 64e-5
DEPTH = 1
DEEPNORM_ALPHA = (2 * DEPTH) ** 0.25

LANES = 128
PAIR = 2 * HEAD_DIM
F32_SUBLANES = 8
BF16_SUBLANES = 16
VMEM_LIMIT = 56 * 1024 * 1024

NEG = -1e30

LORA_PAD = 512
COL_GATE = 0
COL_Q = 4096
COL_K = COL_Q + ATT_WIDTH
COL_V = COL_K + ATT_WIDTH
COL_R = COL_V + ATT_WIDTH
COL_KR = COL_R + RWKV_WIDTH
COL_VR = COL_KR + RWKV_WIDTH
COL_LORA = COL_VR + RWKV_WIDTH
PROJ_COLS = COL_LORA + LORA_PAD

ATT_TQ = 256
ATT_HB = 16
RWKV_TR = 1024
LORA_TR = 1024
RWKV_SCAN_GROUP = 4
MERGE_TM = 512
MOE_RB = 512
MOE_TF = 1024
COMBINE_TM = 256
DMA_UNROLL = 4


def _dot(a, b, precision=None):
    return jnp.dot(a, b, preferred_element_type=f32, precision=precision)


def _dot_nt(a, b):
    return lax.dot_general(a, b, (((1,), (1,)), ((), ())), preferred_element_type=f32)


def _dot_tn(a, b):
    return lax.dot_general(a, b, (((0,), (0,)), ((), ())), preferred_element_type=f32)


def _split_bf16(a):
    hi = a.astype(bf16)
    return hi, (a - hi.astype(f32)).astype(bf16)


def _dot_split_lhs(a, b_bf16):
    hi, lo = _split_bf16(a)
    return _dot(hi, b_bf16) + _dot(lo, b_bf16)


def _dot_split_rhs(a_bf16, b):
    hi, lo = _split_bf16(b)
    return _dot(a_bf16, hi) + _dot(a_bf16, lo)


def _inproj_kernel(x_ref, w_ref, o_ref, xb_ref):
    @pl.when(pl.program_id(1) == 0)
    def _():
        xb_ref[...] = x_ref[...].astype(bf16)

    o_ref[...] = _dot(xb_ref[...], w_ref[...]).astype(o_ref.dtype)


def _in_proj(x2, w, tm, tn):
    M, K = x2.shape
    N = w.shape[1]
    return pl.pallas_call(
        _inproj_kernel,
        out_shape=jax.ShapeDtypeStruct((M, N), bf16),
        grid=(M // tm, N // tn),
        in_specs=[pl.BlockSpec((tm, K), lambda i, j: (i, 0)),
                  pl.BlockSpec((K, tn), lambda i, j: (0, j))],
        out_specs=pl.BlockSpec((tm, tn), lambda i, j: (i, j)),
        scratch_shapes=[pltpu.VMEM((tm, K), bf16)],
        compiler_params=pltpu.CompilerParams(
            dimension_semantics=("parallel", "arbitrary"), vmem_limit_bytes=VMEM_LIMIT),
        name="in_proj",
    )(x2, w)


def _attn_bias_table(rel_bias):
    tq = ATT_TQ
    nk = 3 * tq
    H = rel_bias.shape[0]
    C = CHUNK
    nqc, nkc = tq // C, nk // C
    e = jnp.arange(-(nkc - 1), nqc)
    dist = (e[:, None] * C + jnp.arange(-(C - 1), C)[None, :]) + 2 * tq
    idx = jnp.clip(jnp.minimum(dist, MAX_PAST_DIST) + (C - 1), 0, REL_TABLE - 1)
    w = jnp.pad(rel_bias.astype(f32)[:, idx], ((0, 0), (0, 0), (0, 1)))
    ne = e.shape[0]
    G = jnp.broadcast_to(w[:, :, None, :], (H, ne, C, 2 * C)).reshape(H, ne, C * 2 * C)
    G = G[:, :, :C * (2 * C - 1)].reshape(H, ne, C, 2 * C - 1)
    blocks = jnp.swapaxes(G[:, :, :, C - 1:], 2, 3)
    qc = jnp.arange(nqc)[:, None]
    kc = jnp.arange(nkc)[None, :]
    d = qc + (2 * tq) // C - kc
    tiles = blocks[:, (qc - kc) + (nkc - 1)]
    tiles = jnp.where(((d >= 0) & (d <= LEFT_CHUNKS))[None, :, :, None, None], tiles, NEG)
    return jnp.transpose(tiles, (0, 1, 3, 2, 4)).reshape(H, tq, nk)


def _attn_kernel(q_ref, k0_ref, k1_ref, k2_ref, v0_ref, v1_ref, v2_ref, bias_ref, side_ref,
                 o_ref, side_o_ref):
    qt = pl.program_id(2)
    side_o_ref[...] = side_ref[...].astype(side_o_ref.dtype)
    tq = q_ref.shape[0]
    k_refs = (k0_ref, k1_ref, k2_ref)
    v_refs = (v0_ref, v1_ref, v2_ref)
    heads = range(ATT_HB)
    blocks = range(3)
    lane = lax.broadcasted_iota(i32, (1, PAIR), 1)
    head0 = lane < HEAD_DIM
    pair = [slice((h // 2) * PAIR, (h // 2 + 1) * PAIR) for h in heads]
    q = [jnp.where(head0 if h % 2 == 0 else jnp.logical_not(head0),
                   q_ref[:, pair[h]] * (HEAD_DIM ** -0.5), 0.0).astype(bf16) for h in heads]
    s = [[_dot_nt(q[h], k_refs[j][:, pair[h]]) + bias_ref[h, :, j * tq:(j + 1) * tq] for j in blocks]
         for h in heads]
    s = [[jnp.where(qt - 2 + j >= 0, s[h][j], NEG) if j < 2 else s[h][j] for j in blocks] for h in heads]
    m = [jnp.max(jnp.maximum(jnp.maximum(s[h][0], s[h][1]), s[h][2]), axis=-1, keepdims=True)
         for h in heads]
    p = [[jnp.exp((s[h][j] - m[h]).astype(bf16)) for j in blocks] for h in heads]
    ones = jnp.ones((tq, PAIR), bf16)
    v1 = [[jnp.concatenate([v_refs[j][:, pair[2 * g]], ones], axis=1) for j in blocks]
          for g in range(ATT_HB // 2)]
    acc = [_dot(p[h][0], v1[h // 2][0]) + _dot(p[h][1], v1[h // 2][1]) + _dot(p[h][2], v1[h // 2][2])
           for h in heads]
    out = [acc[h][:, 0:PAIR] / acc[h][:, PAIR:PAIR + 1] for h in heads]
    for g in range(ATT_HB // 2):
        o_ref[:, pair[2 * g]] = jnp.where(head0, out[2 * g], out[2 * g + 1]).astype(o_ref.dtype)


def _attention(proj, bias, B, S, side):
    T = B * S
    tq = ATT_TQ
    nt = S // tq
    width = ATT_HB * HEAD_DIM
    ngrp = ATT_WIDTH // width
    side_spec = pl.BlockSpec((side.shape[0] // (B * nt * ngrp), side.shape[1]),
                             lambda h, b, t: ((h * B + b) * nt + t, 0))
    qcol, kcol, vcol = COL_Q // width, COL_K // width, COL_V // width

    def kv_spec(col, back):
        return pl.BlockSpec((tq, width),
                            lambda h, b, t: (b * nt + jnp.maximum(t - back, 0), col + h))

    return pl.pallas_call(
        _attn_kernel,
        out_shape=(jax.ShapeDtypeStruct((T, ATT_WIDTH), bf16), jax.ShapeDtypeStruct(side.shape, bf16)),
        grid=(ngrp, B, nt),
        in_specs=[pl.BlockSpec((tq, width), lambda h, b, t: (b * nt + t, qcol + h)),
                  kv_spec(kcol, 2), kv_spec(kcol, 1), kv_spec(kcol, 0),
                  kv_spec(vcol, 2), kv_spec(vcol, 1), kv_spec(vcol, 0),
                  pl.BlockSpec((ATT_HB, tq, 3 * tq), lambda h, b, t: (h, 0, 0)),
                  side_spec],
        out_specs=(pl.BlockSpec((tq, width), lambda h, b, t: (b * nt + t, h)), side_spec),
        compiler_params=pltpu.CompilerParams(
            dimension_semantics=("parallel", "parallel", "parallel"), vmem_limit_bytes=VMEM_LIMIT),
        name="chunk_attn",
    )(proj, proj, proj, proj, proj, proj, proj, bias, side)


def _lora_prep_kernel(lo_ref, lop_ref, mu_ref, o_ref):
    t = pl.program_id(1)
    tr = lo_ref.shape[0]
    p = lo_ref[...].astype(f32)
    prev_row = lop_ref[BF16_SUBLANES - 1:BF16_SUBLANES, :].astype(f32) * jnp.where(t == 0, 0.0, 1.0)
    row = lax.broadcasted_iota(i32, (tr, 1), 0)
    prev = jnp.where(row == 0, prev_row, pltpu.roll(p, 1, 0))
    lo = p + (prev - p) * mu_ref[...]
    o_ref[:, 0:LANES] = jnp.tanh(lo[:, 0:LANES]).astype(o_ref.dtype)
    o_ref[:, LANES:2 * LANES] = lo[:, LANES:2 * LANES].astype(o_ref.dtype)
    o_ref[:, 2 * LANES:] = jax.nn.sigmoid(lo[:, 2 * LANES:]).astype(o_ref.dtype)


def _lora_prep(proj, B, S, mu_lora):
    tr = LORA_TR
    nt = S // tr
    sub = tr // BF16_SUBLANES
    nsub = S // BF16_SUBLANES
    locol = COL_LORA // LORA_PAD
    return pl.pallas_call(
        _lora_prep_kernel,
        out_shape=jax.ShapeDtypeStruct((B * S, LORA_PAD), bf16),
        grid=(B, nt),
        in_specs=[pl.BlockSpec((tr, LORA_PAD), lambda b, t: (b * nt + t, locol)),
                  pl.BlockSpec((BF16_SUBLANES, LORA_PAD),
                               lambda b, t: (b * nsub + jnp.maximum(t * sub - 1, 0), locol)),
                  pl.BlockSpec((1, LORA_PAD), lambda b, t: (0, 0))],
        out_specs=pl.BlockSpec((tr, LORA_PAD), lambda b, t: (b * nt + t, 0)),
        compiler_params=pltpu.CompilerParams(
            dimension_semantics=("parallel", "parallel"), vmem_limit_bytes=VMEM_LIMIT),
        name="rwkv7_lora_prep",
    )(proj, proj, mu_lora)


def _rwkv_kernel(r_ref, k_ref, v_ref, act_ref, rp_ref, kp_ref, vp_ref,
                 mur_ref, muk_ref, muv_ref, w0_ref, a0_ref, kkp_ref, kap_ref, rkp_ref,
                 lnw_ref, lnb_ref, wup_ref, aup_ref, gup_ref, side_ref,
                 o_ref, side_o_ref, zero_o_ref,
                 st_ref, r_s, lw_s, k_s, v_s, a_s, b_s, g_s, bon_s):
    t = pl.program_id(2)
    tr = r_ref.shape[0]
    C = CHUNK

    side_o_ref[...] = side_ref[...].astype(side_o_ref.dtype)
    zero_o_ref[...] = jnp.zeros_like(zero_o_ref)

    @pl.when(t == 0)
    def _():
        st_ref[...] = jnp.zeros_like(st_ref)

    not_first = jnp.where(t == 0, 0.0, 1.0).astype(f32)
    row = lax.broadcasted_iota(i32, (tr, 1), 0)

    def shift(p_ref, prev_ref, mu_ref):
        p = p_ref[...].astype(f32)
        prev_row = prev_ref[BF16_SUBLANES - 1:BF16_SUBLANES, :].astype(f32) * not_first
        prev = jnp.where(row == 0, prev_row, pltpu.roll(p, 1, 0))
        return p + (prev - p) * mu_ref[...]

    r = shift(r_ref, rp_ref, mur_ref)
    k = shift(k_ref, kp_ref, muk_ref)
    v = shift(v_ref, vp_ref, muv_ref)

    ri = lax.broadcasted_iota(i32, (PAIR, PAIR), 0)
    ci = lax.broadcasted_iota(i32, (PAIR, PAIR), 1)
    same_head = (ri // HEAD_DIM) == (ci // HEAD_DIM)
    head_ones = jnp.where(same_head, 1.0, 0.0).astype(bf16)
    head_avg = jnp.where(same_head, 1.0 / HEAD_DIM, 0.0).astype(bf16)

    z = w0_ref[...] + _dot(act_ref[:, 0:LANES], wup_ref[...])
    softplus_negz = jnp.maximum(-z, 0.0) + jnp.log(1.0 + jnp.exp(-jnp.abs(z)))
    lw = -jnp.exp(-softplus_negz - 0.5)
    alr = jax.nn.sigmoid(a0_ref[...] + _dot(act_ref[:, LANES:2 * LANES], aup_ref[...]))
    g = _dot(act_ref[:, 2 * LANES:], gup_ref[...])
    kk = k * kkp_ref[...]
    kk = kk / jnp.maximum(jnp.sqrt(_dot_split_lhs(kk * kk, head_ones)), 1e-12)
    k2 = k * (1.0 + (alr - 1.0) * kap_ref[...])
    bonus = _dot_split_lhs(r * k2 * rkp_ref[...], head_ones) * v

    r_s[...] = r
    lw_s[...] = lw
    k_s[...] = k2
    v_s[...] = v
    a_s[...] = -kk
    b_s[...] = kk * alr
    g_s[...] = g
    bon_s[...] = bonus

    lane = lax.broadcasted_iota(i32, (1, PAIR), 1)
    head0 = lane < HEAD_DIM
    tri_incl = jnp.where(lax.broadcasted_iota(i32, (C, C), 1) <= lax.broadcasted_iota(i32, (C, C), 0),
                         1.0, 0.0).astype(bf16)
    strict = ci < ri
    incl = ci <= ri
    eye = ci == ri
    lnw = lnw_ref[...]
    lnb = lnb_ref[...]

    def stack(x):
        return jnp.concatenate([jnp.where(head0, x, 0.0), jnp.where(head0, 0.0, x)], axis=0)

    nc = tr // C
    chunks = range(nc)
    rows = [slice(c * C, (c + 1) * C) for c in chunks]
    lwc = [lw_s[rw, :] for rw in rows]
    L = [_dot_split_rhs(tri_incl, x) for x in lwc]
    LC = [x[C - 1:C, :] for x in L]
    a_t, r_t, r_tb, bk_t, v_st = [], [], [], [], []
    for c in chunks:
        rw = rows[c]
        e_neg = jnp.exp(-L[c])
        a_t.append(stack(a_s[rw, :] * jnp.exp(L[c] - lwc[c])).astype(bf16))
        r_t.append(stack(r_s[rw, :] * jnp.exp(L[c])))
        r_tb.append(r_t[c].astype(bf16))
        bk_t.append(jnp.concatenate([stack(b_s[rw, :] * e_neg), stack(k_s[rw, :] * e_neg)], axis=0).astype(bf16))
        v_st.append(stack(v_s[rw, :]).astype(bf16))

    A = [_dot_nt(jnp.concatenate([a_t[c], r_tb[c]], axis=0), bk_t[c]) for c in chunks]
    a_ab = [jnp.where(strict, x[0:PAIR, 0:PAIR], 0.0) for x in A]
    a_kk = [jnp.concatenate([jnp.where(strict, x[0:PAIR, PAIR:], 0.0),
                             jnp.where(incl, x[PAIR:, PAIR:], 0.0)], axis=0).astype(bf16) for x in A]
    a_rb = [jnp.where(incl, x[PAIR:, 0:PAIR], 0.0).astype(bf16) for x in A]

    def sub_diag(s):
        return jnp.logical_and(jnp.logical_and((ri // (2 * s)) == (ci // (2 * s)), (ri // s) % 2 == 1),
                               (ci // s) % 2 == 0)

    tinv = [jnp.where(eye, 1.0, 0.0).astype(f32) + jnp.where(sub_diag(1), x, 0.0) for x in a_ab]
    s_blk = 2
    while s_blk < HEAD_DIM:
        mask = sub_diag(s_blk)
        n21 = [jnp.where(mask, x, 0.0).astype(bf16) for x in a_ab]
        tb = [tv.astype(bf16) for tv in tinv]
        n21_t11 = [_dot(x, tv).astype(bf16) for x, tv in zip(n21, tb)]
        tinv = [tv + _dot(tvb, x) for tv, tvb, x in zip(tinv, tb, n21_t11)]
        s_blk *= 2

    av = [_dot(a_kk[c], v_st[c]) for c in chunks]
    wu = [_dot(tinv[c].astype(bf16), jnp.concatenate([a_t[c], av[c][0:PAIR].astype(bf16)], axis=1))
          for c in chunks]
    wub = [x.astype(bf16) for x in wu]
    ry = [jnp.concatenate([r_t[c], av[c][PAIR:]], axis=1) + _dot(a_rb[c], wub[c]) for c in chunks]
    ry = [x[0:C] + x[C:] for x in ry]

    mc, gc = [], []
    for c in chunks:
        rw = rows[c]
        e_end = jnp.exp(LC[c] - L[c])
        b_h = stack(b_s[rw, :] * e_end).astype(bf16)
        k_h = stack(k_s[rw, :] * e_end).astype(bf16)
        mc.append(jnp.where(eye, jnp.exp(LC[c]), 0.0) + _dot_tn(wub[c][:, 0:PAIR], b_h))
        gc.append(_dot_tn(jnp.concatenate([wub[c][:, PAIR:], v_st[c]], axis=0),
                          jnp.concatenate([b_h, k_h], axis=0)))

    mcb = [x.astype(bf16) for x in mc]
    gs = RWKV_SCAN_GROUP
    ng = nc // gs
    pq = [jnp.concatenate([mc[g * gs], gc[g * gs]], axis=0) for g in range(ng - 1)]
    for j in range(1, gs):
        pq = [_dot(pq[g].astype(bf16), mcb[g * gs + j])
              + jnp.concatenate([jnp.zeros((PAIR, PAIR), f32), gc[g * gs + j]], axis=0) for g in range(ng - 1)]
    entry = [st_ref[...]]
    for g in range(ng - 1):
        entry.append(_dot(entry[g].astype(bf16), pq[g][0:PAIR].astype(bf16)) + pq[g][PAIR:])
    y = [None] * nc
    cur = entry
    for j in range(gs):
        curb = [x.astype(bf16) for x in cur]
        for g in range(ng):
            c = g * gs + j
            y[c] = _dot_nt(ry[c][:, 0:PAIR].astype(bf16), curb[g]) + ry[c][:, PAIR:]
        if j + 1 < gs:
            cur = [_dot(curb[g], mcb[g * gs + j]) + gc[g * gs + j] for g in range(ng)]
        else:
            st_ref[...] = _dot(curb[ng - 1], mcb[nc - 1]) + gc[nc - 1]

    mu = [_dot(x.astype(bf16), head_avg) for x in y]
    d = [x - m for x, m in zip(y, mu)]
    var = [_dot((x * x).astype(bf16), head_avg) for x in d]
    for c in chunks:
        yn = d[c] * lax.rsqrt(var[c] + GN_EPS) * lnw + lnb
        o_ref[rows[c], :] = ((yn + bon_s[rows[c], :]) * g_s[rows[c], :]).astype(o_ref.dtype)


def _rwkv(proj, act, B, S, mu_rkv, w0, a0, k_k, k_a, r_k, lnx_w, lnx_b, w_up_p, a_up_p, g_up, side,
          zero_shape):
    T = B * S
    tr = RWKV_TR
    nt = S // tr
    sub = tr // BF16_SUBLANES
    nsub = S // BF16_SUBLANES
    npair = RWKV_WIDTH // PAIR
    rcol, kcol, vcol = COL_R // PAIR, COL_KR // PAIR, COL_VR // PAIR
    side_rows = side.shape[0] // (B * npair * nt)
    zero_rows = zero_shape[0] // (B * npair * nt)
    step = lambda b, h, t: ((b * npair + h) * nt + t, 0)

    def cur(col):
        return pl.BlockSpec((tr, PAIR), lambda b, h, t: (b * nt + t, col + h))

    def prev(col):
        return pl.BlockSpec((BF16_SUBLANES, PAIR),
                            lambda b, h, t: (b * nsub + jnp.maximum(t * sub - 1, 0), col + h))

    def vec(off):
        return pl.BlockSpec((1, PAIR), lambda b, h, t: (0, off + h))

    in_specs = [
        cur(rcol), cur(kcol), cur(vcol),
        pl.BlockSpec((tr, LORA_PAD), lambda b, h, t: (b * nt + t, 0)),
        prev(rcol), prev(kcol), prev(vcol),
        vec(0), vec(npair), vec(2 * npair),
        vec(0), vec(0), vec(0), vec(0), vec(0), vec(0), vec(0),
        pl.BlockSpec((LANES, PAIR), lambda b, h, t: (0, h)),
        pl.BlockSpec((LANES, PAIR), lambda b, h, t: (0, h)),
        pl.BlockSpec((GATE_LORA, PAIR), lambda b, h, t: (0, h)),
        pl.BlockSpec((side_rows, side.shape[1]), step),
    ]
    scratch = [pltpu.VMEM((PAIR, PAIR), f32)] + [pltpu.VMEM((tr, PAIR), f32)] * 8
    return pl.pallas_call(
        _rwkv_kernel,
        out_shape=(jax.ShapeDtypeStruct((T, RWKV_WIDTH), bf16),
                   jax.ShapeDtypeStruct(side.shape, bf16),
                   jax.ShapeDtypeStruct(zero_shape, f32)),
        grid=(B, npair, nt),
        in_specs=in_specs,
        out_specs=(pl.BlockSpec((tr, PAIR), lambda b, h, t: (b * nt + t, h)),
                   pl.BlockSpec((side_rows, side.shape[1]), step),
                   pl.BlockSpec((zero_rows, zero_shape[1]), step)),
        scratch_shapes=scratch,
        compiler_params=pltpu.CompilerParams(
            dimension_semantics=("parallel", "parallel", "arbitrary"), vmem_limit_bytes=VMEM_LIMIT),
        name="rwkv7",
    )(proj, proj, proj, act, proj, proj, proj,
      mu_rkv, mu_rkv, mu_rkv, w0, a0, k_k, k_a, r_k, lnx_w, lnx_b, w_up_p, a_up_p, g_up, side)


def _layer_norm(z, w, b):
    mu = jnp.mean(z, axis=-1, keepdims=True)
    d = z - mu
    var = jnp.mean(d * d, axis=-1, keepdims=True)
    return d * lax.rsqrt(var + LN_EPS) * w + b


def _merge_kernel(x_ref, ya_ref, yb_ref, ga_ref, gb_ref, pa_ref, pb_ref, wo_ref, l1w_ref, l1b_ref,
                  wrt_ref, brt_ref,
                  h_ref, topi_ref, gate_ref, rank_ref, cnt_ref,
                  run_ref):
    i = pl.program_id(0)
    tm = x_ref.shape[0]
    ne = wrt_ref.shape[0]

    @pl.when(i == 0)
    def _():
        run_ref[...] = jnp.zeros_like(run_ref)

    halves = [slice(0, tm // 2), slice(tm // 2, tm)]
    ma = [_dot(ya_ref[rw, :], pa_ref[...]) for rw in halves]
    mb = [_dot(yb_ref[rw, :], pb_ref[...]) for rw in halves]
    merged = [(jax.nn.sigmoid(ga_ref[rw, :].astype(f32)) * a
               + jax.nn.sigmoid(gb_ref[rw, :].astype(f32)) * b).astype(bf16)
              for rw, a, b in zip(halves, ma, mb)]
    mix = [_dot(m, wo_ref[...]) for m in merged]
    h = jnp.concatenate([_layer_norm(DEEPNORM_ALPHA * x_ref[rw, :] + mx, l1w_ref[...], l1b_ref[...])
                         for rw, mx in zip(halves, mix)], axis=0)
    h_ref[...] = h

    h_hi, h_lo = _split_bf16(h)
    w_hi, w_lo = _split_bf16(wrt_ref[...])
    prod = _dot_nt(jnp.concatenate([w_hi, w_lo], axis=0), jnp.concatenate([h_hi, h_lo], axis=0))
    lg = prod[0:ne, 0:tm] + (prod[0:ne, tm:] + prod[ne:, 0:tm]) + brt_ref[...]
    eidx = lax.broadcasted_iota(i32, (ne, tm), 0)
    slot = lax.broadcasted_iota(i32, (F32_SUBLANES, tm), 0)
    vals, idxs = [], []
    for _ in range(TOP_K):
        m = jnp.max(lg, axis=0, keepdims=True)
        idx = jnp.min(jnp.where(lg == m, eidx, ne), axis=0, keepdims=True)
        vals.append(m)
        idxs.append(idx)
        lg = jnp.where(eidx == idx, -jnp.inf, lg)
    exps = [jnp.exp(vk - vals[0]) for vk in vals]
    denom = exps[0] + exps[1] + exps[2] + exps[3]

    onehots = [jnp.where(eidx == idx, 1.0, 0.0).astype(f32) for idx in idxs]
    oh = onehots[0] + onehots[1] + onehots[2] + onehots[3]
    tri = jnp.where(lax.broadcasted_iota(i32, (tm, tm), 0) < lax.broadcasted_iota(i32, (tm, tm), 1),
                    1.0, 0.0).astype(bf16)
    before = run_ref[...] + _dot(oh.astype(bf16), tri)

    topi_o = jnp.zeros((F32_SUBLANES, tm), i32)
    gate_o = jnp.zeros((F32_SUBLANES, tm), f32)
    rank_o = jnp.zeros((F32_SUBLANES, tm), i32)
    for kslot in range(TOP_K):
        rk = jnp.sum(onehots[kslot] * before, axis=0, keepdims=True).astype(i32)
        topi_o = jnp.where(slot == kslot, idxs[kslot], topi_o)
        gate_o = jnp.where(slot == kslot, exps[kslot] / denom, gate_o)
        rank_o = jnp.where(slot == kslot, rk, rank_o)
    topi_ref[...] = topi_o
    gate_ref[...] = gate_o
    rank_ref[...] = rank_o
    run = run_ref[...] + jnp.sum(oh, axis=1, keepdims=True)
    run_ref[...] = run
    cnt_ref[...] = run


def _merge_router(x2, ya, yb, proj, pa, pb, wo, l1w, l1b, wrt, brt):
    T, D = x2.shape
    tm = MERGE_TM
    ne = wrt.shape[0]
    gcol = COL_GATE // D

    def const(shape):
        return pl.BlockSpec(shape, lambda i: (0, 0), pipeline_mode=pl.Buffered(1))

    out_shape = (jax.ShapeDtypeStruct((T, D), f32),
                 jax.ShapeDtypeStruct((F32_SUBLANES, T), i32),
                 jax.ShapeDtypeStruct((F32_SUBLANES, T), f32),
                 jax.ShapeDtypeStruct((F32_SUBLANES, T), i32),
                 jax.ShapeDtypeStruct((ne, 1), f32))
    return pl.pallas_call(
        _merge_kernel,
        out_shape=out_shape,
        grid=(T // tm,),
        in_specs=[pl.BlockSpec((tm, D), lambda i: (i, 0)),
                  pl.BlockSpec((tm, ATT_WIDTH), lambda i: (i, 0)),
                  pl.BlockSpec((tm, RWKV_WIDTH), lambda i: (i, 0)),
                  pl.BlockSpec((tm, D), lambda i: (i, gcol)),
                  pl.BlockSpec((tm, D), lambda i: (i, gcol + 1)),
                  const(pa.shape), const(pb.shape), const(wo.shape),
                  const((1, D)), const((1, D)), const(wrt.shape), const((ne, 1))],
        out_specs=(pl.BlockSpec((tm, D), lambda i: (i, 0)),
                   pl.BlockSpec((F32_SUBLANES, tm), lambda i: (0, i)),
                   pl.BlockSpec((F32_SUBLANES, tm), lambda i: (0, i)),
                   pl.BlockSpec((F32_SUBLANES, tm), lambda i: (0, i)),
                   pl.BlockSpec((ne, 1), lambda i: (0, 0))),
        scratch_shapes=[pltpu.VMEM((ne, 1), f32)],
        compiler_params=pltpu.CompilerParams(
            dimension_semantics=("arbitrary",), vmem_limit_bytes=VMEM_LIMIT),
        name="merge_ln1_router",
    )(x2, ya, yb, proj, proj, pa, pb, wo, l1w, l1b, wrt, brt)


def _dispatch_kernel(dest_ref, h_ref, xs_in_hbm, xs_hbm, sem):
    del xs_in_hbm
    tm = h_ref.shape[0]

    def row_copy(t, k):
        return pltpu.make_async_copy(h_ref.at[pl.ds(t, 1)],
                                     xs_hbm.at[pl.ds(dest_ref[t * TOP_K + k], 1)], sem)

    def start(t, c):
        for k in range(TOP_K):
            row_copy(t, k).start()
        return c

    def wait(t, c):
        for k in range(TOP_K):
            row_copy(t, k).wait()
        return c

    lax.fori_loop(0, tm, start, 0, unroll=DMA_UNROLL)
    lax.fori_loop(0, tm, wait, 0, unroll=DMA_UNROLL)


def _dispatch(dest_flat, h, xs0):
    T, D = h.shape
    n = COMBINE_TM * TOP_K
    return pl.pallas_call(
        _dispatch_kernel,
        out_shape=jax.ShapeDtypeStruct(xs0.shape, xs0.dtype),
        grid=(T * TOP_K // n,),
        in_specs=[pl.BlockSpec((n,), lambda i: (i,), memory_space=pltpu.SMEM),
                  pl.BlockSpec((n // TOP_K, D), lambda i: (i, 0)),
                  pl.BlockSpec(memory_space=pl.ANY)],
        out_specs=pl.BlockSpec(memory_space=pl.ANY),
        scratch_shapes=[pltpu.SemaphoreType.DMA(())],
        input_output_aliases={2: 0},
        compiler_params=pltpu.CompilerParams(
            dimension_semantics=("arbitrary",), has_side_effects=True, vmem_limit_bytes=VMEM_LIMIT),
        name="moe_dispatch",
    )(dest_flat, h, xs0)


def _expert_kernel(be_ref, nu_ref, nv_ref, x_ref, wg_ref, wu_ref, bg_ref, bu_ref, wd_ref, bd_ref, o_ref, xb_ref):
    del be_ref
    i = pl.program_id(0)
    j = pl.program_id(1)
    used = i < nu_ref[0]
    half = o_ref.shape[0] // 2
    top_only = nv_ref[i] <= half

    @pl.when(jnp.logical_and(jnp.logical_not(used), j == 0))
    def _():
        o_ref[...] = jnp.zeros_like(o_ref)

    def ffn(rows, first):
        if first:
            xb = x_ref[rows, :].astype(bf16)
            xb_ref[rows, :] = xb
        else:
            xb = xb_ref[rows, :]
        g = _dot(xb, wg_ref[0]) + bg_ref[0]
        u = _dot(xb, wu_ref[0]) + bu_ref[0]
        g = jnp.minimum(g, SWIGLU_LIMIT)
        u = jnp.clip(u, -SWIGLU_LIMIT, SWIGLU_LIMIT)
        act = (u + 1.0) * (g * jax.nn.sigmoid(SWIGLU_ALPHA * g))
        y = _dot(act.astype(bf16), wd_ref[0])
        if first:
            o_ref[rows, :] = y + bd_ref[0]
        else:
            o_ref[rows, :] += y

    for first in (True, False):
        at_step = (j == 0) if first else (j > 0)

        @pl.when(jnp.logical_and(jnp.logical_and(used, at_step), jnp.logical_not(top_only)))
        def _():
            ffn(slice(None), first)

        @pl.when(jnp.logical_and(jnp.logical_and(used, at_step), top_only))
        def _():
            ffn(slice(0, half), first)
            if first:
                o_ref[half:, :] = jnp.zeros((half, o_ref.shape[1]), o_ref.dtype)


def _experts(blk_e, nused, nvalid, xs, w_gu, b_gu, w_down, b_down):
    R, D = xs.shape
    E, _, F2 = w_gu.shape
    F = F2 // 2
    rb, tf = MOE_RB, min(MOE_TF, F)
    nf = F // tf

    def jj(i, j, nu):
        return jnp.where(i < nu[0], j, 0)

    grid_spec = pltpu.PrefetchScalarGridSpec(
        num_scalar_prefetch=3,
        grid=(R // rb, nf),
        in_specs=[pl.BlockSpec((rb, D), lambda i, j, be, nu, nv: (jnp.where(i < nu[0], i, 0), 0)),
                  pl.BlockSpec((1, D, tf), lambda i, j, be, nu, nv: (be[i], 0, jj(i, j, nu))),
                  pl.BlockSpec((1, D, tf), lambda i, j, be, nu, nv: (be[i], 0, nf + jj(i, j, nu))),
                  pl.BlockSpec((1, 1, tf), lambda i, j, be, nu, nv: (be[i], 0, jj(i, j, nu))),
                  pl.BlockSpec((1, 1, tf), lambda i, j, be, nu, nv: (be[i], 0, nf + jj(i, j, nu))),
                  pl.BlockSpec((1, tf, D), lambda i, j, be, nu, nv: (be[i], jj(i, j, nu), 0)),
                  pl.BlockSpec((1, 1, D), lambda i, j, be, nu, nv: (be[i], 0, 0))],
        out_specs=pl.BlockSpec((rb, D), lambda i, j, be, nu, nv: (i, 0)),
        scratch_shapes=[pltpu.VMEM((rb, D), bf16)],
    )
    return pl.pallas_call(
        _expert_kernel,
        out_shape=jax.ShapeDtypeStruct((R, D), f32),
        grid_spec=grid_spec,
        compiler_params=pltpu.CompilerParams(
            dimension_semantics=("parallel", "arbitrary"), vmem_limit_bytes=VMEM_LIMIT),
        name="moe_experts",
    )(blk_e, nused, nvalid, xs, w_gu, w_gu, b_gu, b_gu, w_down, b_down)


def _combine_kernel(dest_ref, dest_next_ref, h_ref, gate_ref, l2w_ref, l2b_ref, y_hbm, o_ref, buf, sem):
    i = pl.program_id(0)
    tm = h_ref.shape[0]
    slot = i % 2

    def row_copy(d_ref, sl, t, k):
        return pltpu.make_async_copy(y_hbm.at[pl.ds(d_ref[t * TOP_K + k], 1)],
                                     buf.at[sl, k, pl.ds(t, 1)], sem.at[sl])

    def start_tile(d_ref, sl):
        def body(t, c):
            for k in range(TOP_K):
                row_copy(d_ref, sl, t, k).start()
            return c
        lax.fori_loop(0, tm, body, 0, unroll=DMA_UNROLL)

    def wait_tile(d_ref, sl):
        def body(t, c):
            for k in range(TOP_K):
                row_copy(d_ref, sl, t, k).wait()
            return c
        lax.fori_loop(0, tm, body, 0, unroll=DMA_UNROLL)

    @pl.when(i == 0)
    def _():
        start_tile(dest_ref, 0)

    @pl.when(i + 1 < pl.num_programs(0))
    def _():
        start_tile(dest_next_ref, 1 - slot)

    wait_tile(dest_ref, slot)

    gate = gate_ref[...]
    ffn = gate[:, 0:1] * buf[slot, 0]
    for kslot in range(1, TOP_K):
        ffn = ffn + gate[:, kslot:kslot + 1] * buf[slot, kslot]
    o_ref[...] = _layer_norm(DEEPNORM_ALPHA * h_ref[...] + ffn, l2w_ref[...], l2b_ref[...])


def _combine(dest_flat, h, gate, l2w, l2b, y):
    T, D = h.shape
    tm = COMBINE_TM
    n = tm * TOP_K
    steps = T // tm
    return pl.pallas_call(
        _combine_kernel,
        out_shape=jax.ShapeDtypeStruct((T, D), f32),
        grid=(steps,),
        in_specs=[pl.BlockSpec((n,), lambda i: (i,), memory_space=pltpu.SMEM),
                  pl.BlockSpec((n,), lambda i: (jnp.minimum(i + 1, steps - 1),), memory_space=pltpu.SMEM),
                  pl.BlockSpec((tm, D), lambda i: (i, 0)),
                  pl.BlockSpec((tm, LANES), lambda i: (i, 0)),
                  pl.BlockSpec((1, D), lambda i: (0, 0)),
                  pl.BlockSpec((1, D), lambda i: (0, 0)),
                  pl.BlockSpec(memory_space=pl.ANY)],
        out_specs=pl.BlockSpec((tm, D), lambda i: (i, 0)),
        scratch_shapes=[pltpu.VMEM((2, TOP_K, tm, D), f32), pltpu.SemaphoreType.DMA((2,))],
        compiler_params=pltpu.CompilerParams(
            dimension_semantics=("arbitrary",), vmem_limit_bytes=VMEM_LIMIT),
        name="moe_combine_ln2",
    )(dest_flat, dest_flat, h, gate, l2w, l2b, y)


def _pack_w_in(w_in):
    o1 = 3 * ATT_WIDTH
    o2 = o1 + 3 * RWKV_WIDTH
    o3 = o2 + DECAY_LORA
    o4 = o3 + ICLR_LORA
    o5 = o4 + GATE_LORA
    padc = lambda w, n: jnp.pad(w, ((0, 0), (0, n - w.shape[1])))
    parts = [w_in[:, o5:], w_in[:, :o2], padc(w_in[:, o2:o3], LANES), padc(w_in[:, o3:o4], LANES),
             w_in[:, o4:o5]]
    return jnp.concatenate(parts, axis=1).astype(bf16)


def _token_mixing(x2, B, S, w_in, rel_bias, shift_mu, w0, w_up, a0, a_up, g_up, k_k, k_a, r_k,
                  lnx_w, lnx_b, side_a, side_b, zero_shape):
    T, D = x2.shape
    wp = _pack_w_in(w_in)
    proj = _in_proj(x2, wp, min(1024, T), 1536)

    y_a, side_a = _attention(proj, _attn_bias_table(rel_bias), B, S, side_a)

    W = RWKV_WIDTH
    row = lambda p: p.reshape(1, -1).astype(f32)
    padv = lambda p, n: jnp.pad(p, (0, n - p.shape[0]))
    mu_rkv = row(shift_mu[:3 * W])
    mu_lora = row(jnp.concatenate([padv(shift_mu[3 * W:3 * W + DECAY_LORA], LANES),
                                   padv(shift_mu[3 * W + DECAY_LORA:3 * W + DECAY_LORA + ICLR_LORA], LANES),
                                   shift_mu[3 * W + DECAY_LORA + ICLR_LORA:]]))
    padr = lambda w: jnp.pad(w, ((0, LANES - w.shape[0]), (0, 0))).astype(bf16)
    act = _lora_prep(proj, B, S, mu_lora)
    y_b, side_b, zeros = _rwkv(proj, act, B, S, mu_rkv, row(w0), row(a0), row(k_k), row(k_a), row(r_k),
                               row(lnx_w), row(lnx_b), padr(w_up), padr(a_up), g_up.astype(bf16), side_b,
                               zero_shape)
    return proj, y_a, y_b, side_a, side_b, zeros


def _moe_routing(topi, rank, counts, n_experts, rb, n_blocks):
    M = topi.shape[0] * TOP_K
    counts = counts.astype(i32)
    padded = (counts + rb - 1) // rb * rb
    pad_ends = jnp.cumsum(padded)
    pad_starts = pad_ends - padded
    onehot = topi[..., None] == jnp.arange(n_experts, dtype=i32)
    dest = (jnp.sum(jnp.where(onehot, pad_starts, 0), axis=-1) + rank).reshape(M)
    blk_start = jnp.arange(n_blocks, dtype=i32)[:, None] * rb
    blk_e = jnp.minimum(jnp.sum((pad_ends[None, :] <= blk_start).astype(i32), axis=-1), n_experts - 1)
    nused = (pad_ends[-1:] // rb).astype(i32)
    nvalid = jnp.clip((pad_starts + counts)[blk_e] - blk_start[:, 0], 0, rb).astype(i32)
    return dest.astype(i32), blk_e, nused, nvalid


def kernel(x, w_in, rel_bias, shift_mu, w0, w_up, a0, a_up, g_up, k_k, k_a, r_k, lnx_w, lnx_b,
           proj_a, proj_b, w_out, ln1_w, ln1_b, w_router, b_router, w_gu, b_gu, w_down, b_down,
           ln2_w, ln2_b):
    B, S, D = x.shape
    T = B * S
    h = x.reshape(T, D)
    for l in range(DEPTH):
        E, F = w_down.shape[1], w_down.shape[2]
        n_blocks = (T * TOP_K + E * (MOE_RB - 1) + MOE_RB - 1) // MOE_RB
        proj, y_a, y_b, w_down_b, w_gu_b, xs0 = _token_mixing(
            h, B, S, w_in[l], rel_bias[l], shift_mu[l], w0[l], w_up[l], a0[l], a_up[l], g_up[l],
            k_k[l], k_a[l], r_k[l], lnx_w[l], lnx_b[l],
            w_down[l].reshape(E * F, D), w_gu[l].reshape(E * D, 2 * F), (n_blocks * MOE_RB, D))
        h1, topi, gate, rank, counts = _merge_router(
            h, y_a, y_b, proj, proj_a[l].astype(bf16), proj_b[l].astype(bf16), w_out[l].astype(bf16),
            ln1_w[l].reshape(1, D), ln1_b[l].reshape(1, D), w_router[l].T, b_router[l].reshape(E, 1))
        gate = jnp.pad(gate[:TOP_K].T, ((0, 0), (0, LANES - TOP_K)))
        dest, blk_e, nused, nvalid = _moe_routing(
            topi[:TOP_K].T, rank[:TOP_K].T, counts[:, 0], E, MOE_RB, n_blocks)
        xs = _dispatch(dest, h1, xs0)
        y = _experts(blk_e, nused, nvalid, xs, w_gu_b.reshape(E, D, 2 * F), b_gu[l].reshape(E, 1, -1),
                     w_down_b.reshape(E, F, D), b_down[l].reshape(E, 1, D))
        h = _combine(dest, h1, gate, ln2_w[l].reshape(1, D), ln2_b[l].reshape(1, D), y)
    return h.reshape(B, S, D)
```
